```python
import math
import jax
import jax.numpy as jnp
from jax import lax
import numpy as np

D_MODEL = 2048
BATCH = 4
SEQ = 2048
DEPTH = 4

DA_HEADS = 8
DA_QK_DIM = 64
DA_V_DIM = 2 * DA_QK_DIM
GDN_HEADS = 8
GDN_K_DIM = 128
GDN_V_DIM = 128
GDN_CHUNK = 64
CONV_WIDTH = 4
FOX_HEADS = 16
FOX_HEAD_DIM = 128
Q_BLOCK = 128
D_FF = 4 * D_MODEL
NORM_EPS = 1e-6
L2_EPS = 1e-6

DA_Q_COLS = DA_HEADS * 2 * DA_QK_DIM
DA_K_COLS = DA_HEADS * 2 * DA_QK_DIM
DA_V_COLS = DA_HEADS * DA_V_DIM
GDN_QK_COLS = GDN_HEADS * GDN_K_DIM
GDN_V_COLS = GDN_HEADS * GDN_V_DIM
GDN_CONV_COLS = 2 * GDN_QK_COLS + GDN_V_COLS
_E1 = DA_Q_COLS
_E2 = _E1 + DA_K_COLS
_E3 = _E2 + DA_V_COLS
_E4 = _E3 + GDN_CONV_COLS
_E5 = _E4 + GDN_V_COLS
_E6 = _E5 + GDN_HEADS
EVEN_IN = _E6 + GDN_HEADS
EVEN_SPLITS = (_E1, _E2, _E3, _E4, _E5, _E6)
EVEN_MIX = DA_V_COLS + GDN_V_COLS
FOX_COLS = FOX_HEADS * FOX_HEAD_DIM
ODD_IN = 3 * FOX_COLS + FOX_HEADS
N_EVEN = (DEPTH + 1) // 2
N_ODD = DEPTH // 2

kernel_name = 'hybrid_diffattn_gdn_fox_trunk'


def rms_norm(x, g):
    xf = x.astype(jnp.float32)
    y = xf * lax.rsqrt(jnp.mean(xf * xf, axis=-1, keepdims=True) + NORM_EPS)
    return (y * g.astype(jnp.float32)).astype(x.dtype)


def _l2norm(t):
    return t * lax.rsqrt(jnp.sum(t * t, axis=-1, keepdims=True) + L2_EPS)


def _split_heads(t, n_heads):
    b, s, _ = t.shape
    return t.reshape(b, s, n_heads, -1).transpose(0, 2, 1, 3)


def _merge_heads(t):
    b, h, s, d = t.shape
    return t.transpose(0, 2, 1, 3).reshape(b, s, h * d)


def causal_depthwise_conv(x, w):
    width = w.shape[0]
    return lax.conv_general_dilated(
        x, w[:, None, :], window_strides=(1,), padding=[(width - 1, 0)],
        dimension_numbers=('NWC', 'WIO', 'NWC'), feature_group_count=x.shape[-1])


def differential_attention(q1, q2, k1, k2, v, lam, slopes):
    seq = q1.shape[2]
    scale = DA_QK_DIM ** -0.5
    outs = []
    for start in range(0, seq, Q_BLOCK):
        end = start + Q_BLOCK
        dist = (jnp.arange(start, end)[:, None] - jnp.arange(end)[None, :]).astype(jnp.float32)
        bias = jnp.where(dist >= 0, -slopes[:, None, None] * dist, -jnp.inf)
        s1 = jnp.einsum('bhqd,bhkd->bhqk', q1[:, :, start:end], k1[:, :, :end]).astype(jnp.float32) * scale + bias
        s2 = jnp.einsum('bhqd,bhkd->bhqk', q2[:, :, start:end], k2[:, :, :end]).astype(jnp.float32) * scale + bias
        p = jax.nn.softmax(s1, axis=-1) - lam * jax.nn.softmax(s2, axis=-1)
        outs.append(jnp.einsum('bhqk,bhkd->bhqd', p.astype(v.dtype), v[:, :, :end]))
    return jnp.concatenate(outs, axis=2)


def forgetting_attention(q, k, v, cum_log_f):
    seq = q.shape[2]
    scale = FOX_HEAD_DIM ** -0.5
    outs = []
    for start in range(0, seq, Q_BLOCK):
        end = start + Q_BLOCK
        causal = jnp.arange(start, end)[:, None] >= jnp.arange(end)[None, :]
        bias = cum_log_f[:, :, start:end, None] - cum_log_f[:, :, None, :end]
        s = jnp.einsum('bhqd,bhkd->bhqk', q[:, :, start:end], k[:, :, :end]).astype(jnp.float32) * scale + bias
        p = jax.nn.softmax(jnp.where(causal, s, -jnp.inf), axis=-1)
        outs.append(jnp.einsum('bhqk,bhkd->bhqd', p.astype(v.dtype), v[:, :, :end]))
    return jnp.concatenate(outs, axis=2)


def gated_delta_rule(q, k, v, g, beta):
    b, h, s, dk = q.shape
    dv = v.shape[-1]
    c = GDN_CHUNK
    n = s // c
    q = q.reshape(b, h, n, c, dk)
    k = k.reshape(b, h, n, c, dk)
    v = v.reshape(b, h, n, c, dv)
    g = jnp.cumsum(g.reshape(b, h, n, c), axis=-1)
    beta = beta.reshape(b, h, n, c)[..., None]
    idx = jnp.arange(c)
    incl = idx[:, None] >= idx[None, :]
    strict = idx[:, None] > idx[None, :]
    decay = jnp.exp(jnp.where(incl, g[..., :, None] - g[..., None, :], -jnp.inf))
    k_beta = k * beta
    a = jnp.where(strict, jnp.einsum('bhnid,bhnjd->bhnij', k_beta, k) * decay, 0.0)
    eye = jnp.eye(c, dtype=jnp.float32)
    t_inv = lax.linalg.triangular_solve(a + eye, jnp.broadcast_to(eye, a.shape),
                                        left_side=True, lower=True, unit_diagonal=True)
    u = jnp.einsum('bhnij,bhnjv->bhniv', t_inv, v * beta)
    w = jnp.einsum('bhnij,bhnjk->bhnik', t_inv, k_beta * jnp.exp(g)[..., None])
    intra = jnp.einsum('bhnid,bhnjd->bhnij', q, k) * decay
    q_dec = q * jnp.exp(g)[..., None]
    g_last = g[..., -1]
    k_dec = k * jnp.exp(g_last[..., None] - g)[..., None]

    def step(state, inp):
        u_c, w_c, intra_c, q_c, k_c, gl_c = inp
        v_new = u_c - jnp.einsum('bhck,bhkv->bhcv', w_c, state)
        o_c = jnp.einsum('bhck,bhkv->bhcv', q_c, state) + jnp.einsum('bhij,bhjv->bhiv', intra_c, v_new)
        state = state * jnp.exp(gl_c)[..., None, None] + jnp.einsum('bhck,bhcv->bhkv', k_c, v_new)
        return state, o_c

    xs = (jnp.moveaxis(u, 2, 0), jnp.moveaxis(w, 2, 0), jnp.moveaxis(intra, 2, 0),
          jnp.moveaxis(q_dec, 2, 0), jnp.moveaxis(k_dec, 2, 0), jnp.moveaxis(g_last, 2, 0))
    state0 = jnp.zeros((b, h, dk, dv), jnp.float32)
    _, o = lax.scan(step, state0, xs)
    return jnp.moveaxis(o, 0, 2).reshape(b, h, s, dv)


def even_mixer(h, w_in, conv_w, lam_q1, lam_k1, lam_q2, lam_k2, da_norm, a_log, dt_bias, gdn_norm, w_out, layer):
    b, s, _ = h.shape
    f32 = jnp.float32
    proj = jnp.einsum('bsd,de->bse', h, w_in)
    da_q, da_k, da_v, gdn_qkv, gdn_z, gdn_a, gdn_b = jnp.split(proj, EVEN_SPLITS, axis=-1)
    q = da_q.reshape(b, s, DA_HEADS, 2, DA_QK_DIM).transpose(3, 0, 2, 1, 4)
    k = da_k.reshape(b, s, DA_HEADS, 2, DA_QK_DIM).transpose(3, 0, 2, 1, 4)
    v = _split_heads(da_v, DA_HEADS)
    lam_init = 0.8 - 0.6 * math.exp(-0.3 * layer)
    lam = (jnp.exp(jnp.dot(lam_q1.astype(f32), lam_k1.astype(f32)))
           - jnp.exp(jnp.dot(lam_q2.astype(f32), lam_k2.astype(f32))) + lam_init)
    slopes = 2.0 ** (-8.0 * jnp.arange(1, DA_HEADS + 1, dtype=f32) / DA_HEADS)
    o_a = differential_attention(q[0], q[1], k[0], k[1], v, lam, slopes)
    o_a = _merge_heads(rms_norm(o_a, da_norm) * (1.0 - lam_init)).astype(h.dtype)
    qkv = jax.nn.silu(causal_depthwise_conv(gdn_qkv, conv_w))
    gq, gk, gv = jnp.split(qkv, (GDN_QK_COLS, 2 * GDN_QK_COLS), axis=-1)
    gq = _l2norm(_split_heads(gq, GDN_HEADS).astype(f32)) * (GDN_K_DIM ** -0.5)
    gk = _l2norm(_split_heads(gk, GDN_HEADS).astype(f32))
    gv = _split_heads(gv, GDN_HEADS).astype(f32)
    log_decay = -jnp.exp(a_log.astype(f32)) * jax.nn.softplus(gdn_a.astype(f32) + dt_bias.astype(f32))
    beta = jax.nn.sigmoid(gdn_b.astype(f32))
    o_b = gated_delta_rule(gq, gk, gv, log_decay.transpose(0, 2, 1), beta.transpose(0, 2, 1))
    z = _split_heads(gdn_z, GDN_HEADS).astype(f32)
    o_b = _merge_heads(rms_norm(o_b, gdn_norm) * jax.nn.silu(z)).astype(h.dtype)
    mixed = jnp.concatenate([o_a, o_b], axis=-1)
    return jnp.einsum('bse,ed->bsd', mixed, w_out)


def odd_mixer(h, w_in, b_f, w_out):
    proj = jnp.einsum('bsd,de->bse', h, w_in)
    q, k, v, f_logit = jnp.split(proj, (FOX_COLS, 2 * FOX_COLS, 3 * FOX_COLS), axis=-1)
    log_f = jax.nn.log_sigmoid(f_logit.astype(jnp.float32) + b_f.astype(jnp.float32))
    cum_log_f = jnp.cumsum(log_f, axis=1).transpose(0, 2, 1)
    o = forgetting_attention(_split_heads(q, FOX_HEADS), _split_heads(k, FOX_HEADS),
                             _split_heads(v, FOX_HEADS), cum_log_f)
    return jnp.einsum('bse,ed->bsd', _merge_heads(o), w_out)


def squared_relu_mlp(h, w_up, w_down):
    a = jax.nn.relu(jnp.einsum('bsd,df->bsf', h, w_up))
    return jnp.einsum('bsf,fd->bsd', a * a, w_down)


def setup_inputs(seed: int = 0) -> dict:
    key = jax.random.key(seed)
    ks = jax.random.split(key, 20)
    f32 = jnp.float32

    def nrm(k, shape, std):
        return std * jax.random.normal(k, shape, f32)

    dt = jnp.exp(jax.random.uniform(ks[12], (N_EVEN, GDN_HEADS), f32, math.log(1e-3), math.log(1e-1)))
    return {
        'x': nrm(ks[0], (BATCH, SEQ, D_MODEL), 1.0),
        'norm_mix': 1.0 + nrm(ks[1], (DEPTH, D_MODEL), 0.02),
        'norm_mlp': 1.0 + nrm(ks[2], (DEPTH, D_MODEL), 0.02),
        'norm_final': 1.0 + nrm(ks[3], (D_MODEL,), 0.02),
        'w_in_even': nrm(ks[4], (N_EVEN, D_MODEL, EVEN_IN), D_MODEL ** -0.5),
        'conv_w': nrm(ks[5], (N_EVEN, CONV_WIDTH, GDN_CONV_COLS), CONV_WIDTH ** -0.5),
        'lam_q1': nrm(ks[6], (N_EVEN, DA_QK_DIM), 0.1),
        'lam_k1': nrm(ks[7], (N_EVEN, DA_QK_DIM), 0.1),
        'lam_q2': nrm(ks[8], (N_EVEN, DA_QK_DIM), 0.1),
        'lam_k2': nrm(ks[9], (N_EVEN, DA_QK_DIM), 0.1),
        'da_norm': 1.0 + nrm(ks[10], (N_EVEN, DA_V_DIM), 0.02),
        'gdn_a_log': jnp.log(jax.random.uniform(ks[11], (N_EVEN, GDN_HEADS), f32, 1.0, 16.0)),
        'gdn_dt_bias': dt + jnp.log(-jnp.expm1(-dt)),
        'gdn_norm': 1.0 + nrm(ks[13], (N_EVEN, GDN_V_DIM), 0.02),
        'w_out_even': nrm(ks[14], (N_EVEN, EVEN_MIX, D_MODEL), EVEN_MIX ** -0.5),
        'w_in_odd': nrm(ks[15], (N_ODD, D_MODEL, ODD_IN), D_MODEL ** -0.5),
        'fox_b_f': nrm(ks[16], (N_ODD, FOX_HEADS), 0.1),
        'w_out_odd': nrm(ks[17], (N_ODD, FOX_COLS, D_MODEL), FOX_COLS ** -0.5),
        'w_up': nrm(ks[18], (DEPTH, D_MODEL, D_FF), D_MODEL ** -0.5),
        'w_down': nrm(ks[19], (DEPTH, D_FF, D_MODEL), D_FF ** -0.5),
    }


def reference(x, norm_mix, norm_mlp, norm_final, w_in_even, conv_w, lam_q1, lam_k1, lam_q2, lam_k2,
              da_norm, gdn_a_log, gdn_dt_bias, gdn_norm, w_out_even, w_in_odd, fox_b_f, w_out_odd,
              w_up, w_down):
    h = x
    for layer in range(DEPTH):
        i = layer // 2
        hn = rms_norm(h, norm_mix[layer])
        if layer % 2 == 0:
            h = h + even_mixer(hn, w_in_even[i], conv_w[i], lam_q1[i], lam_k1[i], lam_q2[i], lam_k2[i],
                               da_norm[i], gdn_a_log[i], gdn_dt_bias[i], gdn_norm[i], w_out_even[i], layer)
        else:
            h = h + odd_mixer(hn, w_in_odd[i], fox_b_f[i], w_out_odd[i])
        h = h + squared_relu_mlp(rms_norm(h, norm_mlp[layer]), w_up[layer], w_down[layer])
    return rms_norm(h, norm_final)
```

```python
import functools
import math

import jax
import jax.numpy as jnp
from jax import lax
from jax.experimental import pallas as pl
from jax.experimental.pallas import tpu as pltpu

F32 = jnp.float32
BF16 = jnp.bfloat16

D_MODEL = 2048
BATCH = 4
SEQ = 2048
DEPTH = 4
TOKENS = BATCH * SEQ

DA_HEADS = 8
DA_QK_DIM = 64
DA_V_DIM = 128
GDN_HEADS = 8
GDN_K_DIM = 128
GDN_V_DIM = 128
GDN_CHUNK = 64
CONV_WIDTH = 4
FOX_HEADS = 16
FOX_HEAD_DIM = 128
D_FF = 4 * D_MODEL
NORM_EPS = 1e-6
L2_EPS = 1e-6

LANES = 128
VMEM_LIMIT = 56 * 1024 * 1024

EVEN_MAIN = 3 * 1024 + 3 * 1024 + 1024
DA_Q_BLK, DA_K_BLK, DA_V_BLK = 0, 8, 16
GDN_Q_BLK, GDN_K_BLK, GDN_V_BLK, GDN_Z_BLK = 24, 32, 40, 48
ODD_MAIN = 3 * FOX_HEADS * FOX_HEAD_DIM
FOX_K_BLK, FOX_V_BLK = 16, 32

PROJ_TM, PROJ_TN = 1024, 512
MLP_TM, MLP_TF = 1024, 512
OUT_TM, OUT_TN = 1024, 512
GATE_TM = 256
FOX_TQ, FOX_TK = 256, 256
DA_TQ, DA_TK = 128, 256
N_CHUNKS = SEQ // GDN_CHUNK

NT_DIMS = (((1,), (1,)), ((), ()))
TN_DIMS = (((0,), (0,)), ((), ()))


def _params(*sem):
    return pltpu.CompilerParams(dimension_semantics=sem, vmem_limit_bytes=VMEM_LIMIT)


def _rms_rows(x, g, eps):
    ms = jnp.mean(x * x, axis=-1, keepdims=True)
    return x * lax.rsqrt(ms + eps) * g


def _dot_bf16(a, b, dims=None):
    a = a.astype(BF16)
    b = b.astype(BF16)
    if dims is None:
        return jnp.dot(a, b, preferred_element_type=F32)
    return lax.dot_general(a, b, dims, preferred_element_type=F32)


def _dot_f32(a, b, dims=None):
    if dims is None:
        dims = (((1,), (0,)), ((), ()))
    return lax.dot_general(a, b, dims, precision=lax.Precision.HIGHEST,
                           preferred_element_type=F32)


def _softplus(x):
    return jnp.maximum(x, 0.0) + jnp.log1p(jnp.exp(-jnp.abs(x)))


def _sigmoid(x):
    return 1.0 / (1.0 + jnp.exp(-x))


def _proj_kernel(x_ref, g_ref, w_ref, ws_ref, o_ref, os_ref, xn_ref):
    @pl.when(pl.program_id(1) == 0)
    def _():
        xn = _rms_rows(x_ref[...], g_ref[...], NORM_EPS).astype(BF16)
        xn_ref[...] = xn
        os_ref[...] = jnp.dot(xn, ws_ref[...], preferred_element_type=F32)

    o_ref[...] = jnp.dot(xn_ref[...], w_ref[...], preferred_element_type=F32).astype(o_ref.dtype)


def _in_proj(h, g, w_main, w_side):
    n = w_main.shape[1]
    grid = (TOKENS // PROJ_TM, n // PROJ_TN)
    return pl.pallas_call(
        _proj_kernel,
        grid=grid,
        in_specs=[
            pl.BlockSpec((PROJ_TM, D_MODEL), lambda i, j: (i, 0)),
            pl.BlockSpec((1, D_MODEL), lambda i, j: (0, 0)),
            pl.BlockSpec((D_MODEL, PROJ_TN), lambda i, j: (0, j)),
            pl.BlockSpec((D_MODEL, LANES), lambda i, j: (0, 0)),
        ],
        out_specs=[
            pl.BlockSpec((PROJ_TM, PROJ_TN), lambda i, j: (i, j)),
            pl.BlockSpec((PROJ_TM, LANES), lambda i, j: (i, 0)),
        ],
        out_shape=[
            jax.ShapeDtypeStruct((TOKENS, n), BF16),
            jax.ShapeDtypeStruct((TOKENS, LANES), F32),
        ],
        scratch_shapes=[pltpu.VMEM((PROJ_TM, D_MODEL), BF16)],
        compiler_params=_params("arbitrary", "arbitrary"),
        name="in_proj",
    )(h, g, w_main, w_side)


def _tri_ones(n, chunk):
    r = lax.broadcasted_iota(jnp.int32, (n, n), 0)
    c = lax.broadcasted_iota(jnp.int32, (n, n), 1)
    keep = (c <= r) & (jnp.bitwise_xor(r, c) < chunk)
    return jnp.where(keep, 1.0, 0.0).astype(F32)


def _fox_gate_kernel(s_ref, bf_ref, c_ref, carry_ref):
    @pl.when(pl.program_id(1) == 0)
    def _():
        carry_ref[...] = jnp.zeros_like(carry_ref)

    x = s_ref[...] + bf_ref[...]
    log_f = jnp.minimum(x, 0.0) - jnp.log1p(jnp.exp(-jnp.abs(x)))
    c = _dot_f32(_tri_ones(GATE_TM, GATE_TM), log_f) + carry_ref[...]
    c_ref[...] = c
    carry_ref[...] = c[GATE_TM - 1:GATE_TM, :]


def _fox_gates(side, b_f_lanes):
    return pl.pallas_call(
        _fox_gate_kernel,
        grid=(BATCH, SEQ // GATE_TM),
        in_specs=[
            pl.BlockSpec((GATE_TM, LANES), lambda b, t: (b * (SEQ // GATE_TM) + t, 0)),
            pl.BlockSpec((1, LANES), lambda b, t: (0, 0)),
        ],
        out_specs=pl.BlockSpec((GATE_TM, LANES), lambda b, t: (b * (SEQ // GATE_TM) + t, 0)),
        out_shape=jax.ShapeDtypeStruct((TOKENS, LANES), F32),
        scratch_shapes=[pltpu.VMEM((1, LANES), F32)],
        compiler_params=_params("arbitrary", "arbitrary"),
        name="fox_gates",
    )(side, b_f_lanes)


def _gdn_gate_kernel(s_ref, alog_ref, dtb_ref, o_ref):
    x = s_ref[...]
    log_decay = -jnp.exp(alog_ref[...]) * _softplus(x + dtb_ref[...])
    g = _dot_f32(_tri_ones(GATE_TM, GDN_CHUNK), log_decay)
    lane = lax.broadcasted_iota(jnp.int32, x.shape, 1)
    o_ref[...] = jnp.where(lane < GDN_HEADS, g, _sigmoid(x))


def _gdn_gates(side, a_log_lanes, dt_bias_lanes):
    return pl.pallas_call(
        _gdn_gate_kernel,
        grid=(TOKENS // GATE_TM,),
        in_specs=[
            pl.BlockSpec((GATE_TM, LANES), lambda t: (t, 0)),
            pl.BlockSpec((1, LANES), lambda t: (0, 0)),
            pl.BlockSpec((1, LANES), lambda t: (0, 0)),
        ],
        out_specs=pl.BlockSpec((GATE_TM, LANES), lambda t: (t, 0)),
        out_shape=jax.ShapeDtypeStruct((TOKENS, LANES), F32),
        compiler_params=_params("arbitrary"),
        name="gdn_gates",
    )(side, a_log_lanes, dt_bias_lanes)


def _lane_pick(tile, lane_idx):
    lane = lax.broadcasted_iota(jnp.int32, tile.shape, 1)
    return jnp.sum(jnp.where(lane == lane_idx, tile, 0.0), axis=1, keepdims=True)


def _flash(q, k_ref, v_ref, cq, ck_of, q0, tq, tk, scale):
    rows = q.shape[0]
    d_v = v_ref.shape[-1]
    qpos = q0 + (lax.broadcasted_iota(jnp.int32, (rows, tk), 0) & (tq - 1))
    kcol = lax.broadcasted_iota(jnp.int32, (rows, tk), 1)

    def block(j, carry, masked):
        m, l, acc = carry
        k0 = pl.multiple_of(j * tk, tk)
        k = k_ref[pl.ds(k0, tk), :]
        v = v_ref[pl.ds(k0, tk), :]
        s = lax.dot_general(q, k, NT_DIMS, preferred_element_type=F32)
        if scale != 1.0:
            s = s * scale
        s = s + (cq - ck_of(j))
        if masked:
            s = jnp.where(qpos >= kcol + k0, s, -jnp.inf)
        m_new = jnp.maximum(m, jnp.max(s, axis=1, keepdims=True))
        alpha = jnp.exp(m - m_new)
        p = jnp.exp(s - m_new)
        l = alpha * l + jnp.sum(p, axis=1, keepdims=True)
        acc = alpha * acc + jnp.dot(p.astype(BF16), v, preferred_element_type=F32)
        return m_new, l, acc

    init = (jnp.full((rows, 1), -jnp.inf, F32), jnp.zeros((rows, 1), F32),
            jnp.zeros((rows, d_v), F32))
    n_full = (q0 + 1) // tk
    n_total = (q0 + tq - 1) // tk + 1
    carry = lax.fori_loop(0, n_full, lambda j, c: block(j, c, False), init)
    carry = lax.fori_loop(n_full, n_total, lambda j, c: block(j, c, True), carry)
    _, l, acc = carry
    return acc / l


def _fox_attn_kernel(q_ref, k_ref, v_ref, ctm_ref, chm_ref, o_ref):
    h = pl.program_id(1)
    q0 = pl.program_id(2) * FOX_TQ
    cq = _lane_pick(ctm_ref[...], h)

    def ck_of(j):
        return chm_ref[h, pl.ds(j, 1), :]

    o = _flash(q_ref[...], k_ref, v_ref, cq, ck_of, q0, FOX_TQ, FOX_TK, FOX_HEAD_DIM ** -0.5)
    o_ref[...] = o.astype(o_ref.dtype)


def _fox_attention(qkv, c_tm, c_hm):
    nq = SEQ // FOX_TQ
    return pl.pallas_call(
        _fox_attn_kernel,
        grid=(BATCH, FOX_HEADS, nq),
        in_specs=[
            pl.BlockSpec((FOX_TQ, LANES), lambda b, h, i: (b * nq + i, h)),
            pl.BlockSpec((SEQ, LANES), lambda b, h, i: (b, FOX_K_BLK + h)),
            pl.BlockSpec((SEQ, LANES), lambda b, h, i: (b, FOX_V_BLK + h)),
            pl.BlockSpec((FOX_TQ, LANES), lambda b, h, i: (b * nq + i, 0)),
            pl.BlockSpec((FOX_HEADS, SEQ // FOX_TK, FOX_TK), lambda b, h, i: (0, b, 0)),
        ],
        out_specs=pl.BlockSpec((FOX_TQ, LANES), lambda b, h, i: (b * nq + i, h)),
        out_shape=jax.ShapeDtypeStruct((TOKENS, FOX_HEADS * FOX_HEAD_DIM), BF16),
        compiler_params=_params("arbitrary", "arbitrary", "arbitrary"),
        name="fox_attention",
    )(qkv, qkv, qkv, c_tm, c_hm)


def _da_attn_kernel(slope_ref, q_ref, k_ref, v_ref, lam_ref, norm_ref, o_ref, *, lam_init):
    h = pl.program_id(1)
    q0 = pl.program_id(2) * DA_TQ
    neg_slope = -slope_ref[h]

    q = q_ref[...] * jnp.asarray(DA_QK_DIM ** -0.5, BF16)
    lane = lax.broadcasted_iota(jnp.int32, q.shape, 1)
    zero = jnp.zeros_like(q)
    q2x = jnp.concatenate([jnp.where(lane < DA_QK_DIM, q, zero),
                           jnp.where(lane >= DA_QK_DIM, q, zero)], axis=0)

    rowpos = q0 + (lax.broadcasted_iota(jnp.int32, (2 * DA_TQ, 1), 0) & (DA_TQ - 1))
    cq = neg_slope * rowpos.astype(F32)
    kcol = lax.broadcasted_iota(jnp.int32, (1, DA_TK), 1)

    def ck_of(j):
        return neg_slope * (kcol + j * DA_TK).astype(F32)

    o2 = _flash(q2x, k_ref, v_ref, cq, ck_of, q0, DA_TQ, DA_TK, 1.0)

    lv = lam_ref[...]
    lam = (jnp.exp(jnp.sum(lv[0:1] * lv[1:2], axis=1, keepdims=True))
           - jnp.exp(jnp.sum(lv[2:3] * lv[3:4], axis=1, keepdims=True)) + lam_init)
    o = o2[:DA_TQ] - lam * o2[DA_TQ:]
    o = _rms_rows(o, norm_ref[...], NORM_EPS) * (1.0 - lam_init)
    o_ref[...] = o.astype(o_ref.dtype)


def _da_attention(proj, slopes, lam_vecs, da_norm, lam_init):
    nq = SEQ // DA_TQ
    return pl.pallas_call(
        functools.partial(_da_attn_kernel, lam_init=lam_init),
        grid=(BATCH, DA_HEADS, nq),
        in_specs=[
            pl.BlockSpec(memory_space=pltpu.SMEM),
            pl.BlockSpec((DA_TQ, LANES), lambda b, h, i: (b * nq + i, DA_Q_BLK + h)),
            pl.BlockSpec((SEQ, LANES), lambda b, h, i: (b, DA_K_BLK + h)),
            pl.BlockSpec((SEQ, LANES), lambda b, h, i: (b, DA_V_BLK + h)),
            pl.BlockSpec((4, DA_QK_DIM), lambda b, h, i: (0, 0)),
            pl.BlockSpec((1, DA_V_DIM), lambda b, h, i: (0, 0)),
        ],
        out_specs=pl.BlockSpec((DA_TQ, LANES), lambda b, h, i: (b * nq + i, h)),
        out_shape=jax.ShapeDtypeStruct((TOKENS, DA_HEADS * DA_V_DIM), BF16),
        compiler_params=_params("arbitrary", "arbitrary", "arbitrary"),
        name="da_attention",
    )(slopes, proj, proj, proj, lam_vecs, da_norm)


def _conv_silu(x_ref, w_ref):
    x = x_ref[...].astype(F32)
    w = w_ref[...]
    row = lax.broadcasted_iota(jnp.int32, x.shape, 0)
    y = x * w[CONV_WIDTH - 1:CONV_WIDTH]
    for back in range(1, CONV_WIDTH):
        shifted = jnp.where(row >= back, pltpu.roll(x, back, axis=0), 0.0)
        y = y + shifted * w[CONV_WIDTH - 1 - back:CONV_WIDTH - back]
    return y * _sigmoid(y)


def _l2norm_rows(t):
    return t * lax.rsqrt(jnp.sum(t * t, axis=-1, keepdims=True) + L2_EPS)


def _gdn_kernel(q_ref, k_ref, v_ref, z_ref, wq_ref, wk_ref, wv_ref, gtm_ref, ghm_ref, norm_ref,
                o_ref, qs_ref, ks_ref, vs_ref, os_ref):
    h = pl.program_id(1)
    c = GDN_CHUNK
    qs_ref[...] = _l2norm_rows(_conv_silu(q_ref, wq_ref)) * (GDN_K_DIM ** -0.5)
    ks_ref[...] = _l2norm_rows(_conv_silu(k_ref, wk_ref))
    vs_ref[...] = _conv_silu(v_ref, wv_ref)

    ii = lax.broadcasted_iota(jnp.int32, (c, c), 0)
    jj = lax.broadcasted_iota(jnp.int32, (c, c), 1)
    eye = jnp.where(ii == jj, 1.0, 0.0).astype(F32)

    def chunk(n, state):
        r0 = pl.multiple_of(n * c, c)
        q = qs_ref[pl.ds(r0, c), :]
        k = ks_ref[pl.ds(r0, c), :]
        v = vs_ref[pl.ds(r0, c), :]
        gates = gtm_ref[pl.ds(r0, c), :]
        g_col = _lane_pick(gates, h)
        beta = _lane_pick(gates, GDN_HEADS + h)
        g_row = ghm_ref[h, pl.ds(n, 1), :]
        g_last = g_row[:, c - 1:c]

        decay = jnp.exp(jnp.where(ii >= jj, g_col - g_row, -jnp.inf))
        eg = jnp.exp(g_col)
        k_beta = k * beta
        a = jnp.where(ii > jj, _dot_f32(k_beta, k, NT_DIMS) * decay, 0.0)
        p = -a
        t_inv = eye + p
        for _ in range(5):
            p = _dot_f32(p, p)
            t_inv = t_inv + _dot_f32(t_inv, p)
        u = _dot_bf16(t_inv, v * beta)
        w = _dot_bf16(t_inv, k_beta * eg)
        intra = _dot_bf16(q, k, NT_DIMS) * decay
        q_dec = q * eg
        k_dec = k * jnp.exp(g_last - g_col)

        v_new = u - _dot_bf16(w, state)
        os_ref[pl.ds(r0, c), :] = _dot_bf16(q_dec, state) + _dot_bf16(intra, v_new)
        return state * jnp.exp(g_last) + _dot_bf16(k_dec, v_new, TN_DIMS)

    lax.fori_loop(0, N_CHUNKS, chunk, jnp.zeros((GDN_K_DIM, GDN_V_DIM), F32))

    z = z_ref[...].astype(F32)
    o = _rms_rows(os_ref[...], norm_ref[...], NORM_EPS) * (z * _sigmoid(z))
    o_ref[...] = o.astype(o_ref.dtype)


def _gdn(proj, conv_w, gates_tm, g_hm, gdn_norm):
    seq_blk = lambda off: pl.BlockSpec((SEQ, LANES), lambda b, h: (b, off + h))
    conv_blk = lambda off: pl.BlockSpec((CONV_WIDTH, LANES), lambda b, h: (0, off + h))
    return pl.pallas_call(
        _gdn_kernel,
        grid=(BATCH, GDN_HEADS),
        in_specs=[
            seq_blk(GDN_Q_BLK), seq_blk(GDN_K_BLK), seq_blk(GDN_V_BLK), seq_blk(GDN_Z_BLK),
            conv_blk(0), conv_blk(GDN_HEADS), conv_blk(2 * GDN_HEADS),
            pl.BlockSpec((SEQ, LANES), lambda b, h: (b, 0)),
            pl.BlockSpec((GDN_HEADS, N_CHUNKS, GDN_CHUNK), lambda b, h: (0, b, 0)),
            pl.BlockSpec((1, GDN_V_DIM), lambda b, h: (0, 0)),
        ],
        out_specs=pl.BlockSpec((SEQ, LANES), lambda b, h: (b, h)),
        out_shape=jax.ShapeDtypeStruct((TOKENS, GDN_HEADS * GDN_V_DIM), BF16),
        scratch_shapes=[pltpu.VMEM((SEQ, LANES), F32) for _ in range(4)],
        compiler_params=_params("arbitrary", "arbitrary"),
        name="gdn",
    )(proj, proj, proj, proj, conv_w, conv_w, conv_w, gates_tm, g_hm, gdn_norm)


def _out_proj_kernel(xa_ref, xb_ref, wa_ref, wb_ref, h_ref, o_ref):
    o_ref[...] = (h_ref[...]
                  + jnp.dot(xa_ref[...], wa_ref[...], preferred_element_type=F32)
                  + jnp.dot(xb_ref[...], wb_ref[...], preferred_element_type=F32))


def _out_proj(xa, xb_spec_col, xb, w_out, h):
    half = w_out.shape[0] // 2
    return pl.pallas_call(
        _out_proj_kernel,
        grid=(TOKENS // OUT_TM, D_MODEL // OUT_TN),
        in_specs=[
            pl.BlockSpec((OUT_TM, half), lambda i, j: (i, 0)),
            pl.BlockSpec((OUT_TM, half), lambda i, j: (i, xb_spec_col)),
            pl.BlockSpec((half, OUT_TN), lambda i, j: (0, j)),
            pl.BlockSpec((half, OUT_TN), lambda i, j: (1, j)),
            pl.BlockSpec((OUT_TM, OUT_TN), lambda i, j: (i, j)),
        ],
        out_specs=pl.BlockSpec((OUT_TM, OUT_TN), lambda i, j: (i, j)),
        out_shape=jax.ShapeDtypeStruct((TOKENS, D_MODEL), F32),
        compiler_params=_params("arbitrary", "arbitrary"),
        name="out_proj",
    )(xa, xb, w_out, w_out, h)


def _mlp_kernel(h_ref, g_ref, wu_ref, wd_ref, gf_ref, o_ref, xn_ref, *, final_norm):
    f = pl.program_id(1)

    @pl.when(f == 0)
    def _():
        x = h_ref[...]
        xn_ref[...] = _rms_rows(x, g_ref[...], NORM_EPS).astype(BF16)
        o_ref[...] = x

    a = jnp.maximum(jnp.dot(xn_ref[...], wu_ref[...], preferred_element_type=F32), 0.0)
    o_ref[...] += jnp.dot((a * a).astype(BF16), wd_ref[...], preferred_element_type=F32)

    if final_norm:
        @pl.when(f == pl.num_programs(1) - 1)
        def _():
            o_ref[...] = _rms_rows(o_ref[...], gf_ref[...], NORM_EPS)


def _mlp(h, g, w_up, w_down, g_final, final_norm):
    return pl.pallas_call(
        functools.partial(_mlp_kernel, final_norm=final_norm),
        grid=(TOKENS // MLP_TM, D_FF // MLP_TF),
        in_specs=[
            pl.BlockSpec((MLP_TM, D_MODEL), lambda i, f: (i, 0)),
            pl.BlockSpec((1, D_MODEL), lambda i, f: (0, 0)),
            pl.BlockSpec((D_MODEL, MLP_TF), lambda i, f: (0, f)),
            pl.BlockSpec((MLP_TF, D_MODEL), lambda i, f: (f, 0)),
            pl.BlockSpec((1, D_MODEL), lambda i, f: (0, 0)),
        ],
        out_specs=pl.BlockSpec((MLP_TM, D_MODEL), lambda i, f: (i, 0)),
        out_shape=jax.ShapeDtypeStruct((TOKENS, D_MODEL), F32),
        scratch_shapes=[pltpu.VMEM((MLP_TM, D_MODEL), BF16)],
        compiler_params=_params("arbitrary", "arbitrary"),
        name="mlp",
    )(h, g, w_up, w_down, g_final)


def _lanes(vec, offset=0):
    return jnp.zeros((1, LANES), F32).at[0, offset:offset + vec.shape[0]].set(vec.astype(F32))


def _side_weight(w):
    return jnp.pad(w, ((0, 0), (0, LANES - w.shape[1]))).astype(BF16)


def _even_mixer(h, g_mix, w_in, conv_w, lam_vecs, da_norm, a_log, dt_bias, gdn_norm, w_out, layer):
    proj, side = _in_proj(h, g_mix, w_in[:, :EVEN_MAIN].astype(BF16), _side_weight(w_in[:, EVEN_MAIN:]))
    lam_init = 0.8 - 0.6 * math.exp(-0.3 * layer)
    slopes = 2.0 ** (-8.0 * jnp.arange(1, DA_HEADS + 1, dtype=F32) / DA_HEADS)
    o_a = _da_attention(proj, slopes, lam_vecs, da_norm.reshape(1, DA_V_DIM), lam_init)

    gates_tm = _gdn_gates(side, _lanes(a_log), _lanes(dt_bias))
    g_hm = gates_tm[:, :GDN_HEADS].T.reshape(GDN_HEADS, BATCH * N_CHUNKS, GDN_CHUNK)
    o_b = _gdn(proj, conv_w, gates_tm, g_hm, gdn_norm.reshape(1, GDN_V_DIM))
    return _out_proj(o_a, 0, o_b, w_out.astype(BF16), h)


def _odd_mixer(h, g_mix, w_in, b_f, w_out):
    qkv, side = _in_proj(h, g_mix, w_in[:, :ODD_MAIN].astype(BF16), _side_weight(w_in[:, ODD_MAIN:]))
    c_tm = _fox_gates(side, _lanes(b_f))
    c_hm = c_tm[:, :FOX_HEADS].T.reshape(FOX_HEADS, BATCH * (SEQ // FOX_TK), FOX_TK)
    o = _fox_attention(qkv, c_tm, c_hm)
    return _out_proj(o, 1, o, w_out.astype(BF16), h)


def kernel(x, norm_mix, norm_mlp, norm_final, w_in_even, conv_w, lam_q1, lam_k1, lam_q2, lam_k2,
           da_norm, gdn_a_log, gdn_dt_bias, gdn_norm, w_out_even, w_in_odd, fox_b_f, w_out_odd,
           w_up, w_down):
    h = x.reshape(TOKENS, D_MODEL)
    g_final = norm_final.reshape(1, D_MODEL)
    for layer in range(DEPTH):
        i = layer // 2
        g_mix = norm_mix[layer].reshape(1, D_MODEL)
        if layer % 2 == 0:
            lam_vecs = jnp.stack([lam_q1[i], lam_k1[i], lam_q2[i], lam_k2[i]]).astype(F32)
            h = _even_mixer(h, g_mix, w_in_even[i], conv_w[i], lam_vecs, da_norm[i], gdn_a_log[i],
                            gdn_dt_bias[i], gdn_norm[i], w_out_even[i], layer)
        else:
            h = _odd_mixer(h, g_mix, w_in_odd[i], fox_b_f[i], w_out_odd[i])
        h = _mlp(h, norm_mlp[layer].reshape(1, D_MODEL), w_up[layer].astype(BF16),
                 w_down[layer].astype(BF16), g_final, layer == DEPTH - 1)
    return h.reshape(BATCH, SEQ, D_MODEL)
```

```python
import functools
import math

import jax
import jax.numpy as jnp
from jax import lax
from jax.experimental import pallas as pl
from jax.experimental.pallas import tpu as pltpu

F32 = jnp.float32
BF16 = jnp.bfloat16

D_MODEL = 2048
BATCH = 4
SEQ = 2048
DEPTH = 4
TOKENS = BATCH * SEQ

DA_HEADS = 8
DA_QK_DIM = 64
DA_V_DIM = 128
GDN_HEADS = 8
GDN_K_DIM = 128
GDN_V_DIM = 128
GDN_CHUNK = 64
CONV_WIDTH = 4
FOX_HEADS = 16
FOX_HEAD_DIM = 128
D_FF = 4 * D_MODEL
NORM_EPS = 1e-6
L2_EPS = 1e-6

LANES = 128
VMEM_LIMIT = 56 * 1024 * 1024

EVEN_MAIN = 3 * 1024 + 3 * 1024 + 1024
DA_Q_BLK, DA_K_BLK, DA_V_BLK = 0, 8, 16
GDN_Q_BLK, GDN_K_BLK, GDN_V_BLK, GDN_Z_BLK = 24, 32, 40, 48
ODD_MAIN = 3 * FOX_HEADS * FOX_HEAD_DIM
FOX_K_BLK, FOX_V_BLK = 16, 32

PROJ_TM, PROJ_TN = 1024, 512
MLP_TM, MLP_TF = 1024, 512
OUT_TM, OUT_TN = 1024, 512
GATE_TM = 256
FOX_TQ, FOX_TK = 256, 256
DA_TQ, DA_TK = 256, 256
N_CHUNKS = SEQ // GDN_CHUNK
GDN_GROUP = 256
CHUNKS_PER_GROUP = GDN_GROUP // GDN_CHUNK
N_GROUPS = SEQ // GDN_GROUP

LOG2E = math.log2(math.e)

NT_DIMS = (((1,), (1,)), ((), ()))
TN_DIMS = (((0,), (0,)), ((), ()))


def _params(*sem):
    return pltpu.CompilerParams(dimension_semantics=sem, vmem_limit_bytes=VMEM_LIMIT)


def _rms_rows(x, g, eps):
    ms = jnp.mean(x * x, axis=-1, keepdims=True)
    return x * lax.rsqrt(ms + eps) * g


def _dot_bf16(a, b, dims=None):
    a = a.astype(BF16)
    b = b.astype(BF16)
    if dims is None:
        return jnp.dot(a, b, preferred_element_type=F32)
    return lax.dot_general(a, b, dims, preferred_element_type=F32)


def _dot_f32(a, b):
    return jnp.dot(a, b, precision=lax.Precision.HIGHEST, preferred_element_type=F32)


def _softplus(x):
    return jnp.maximum(x, 0.0) + jnp.log1p(jnp.exp(-jnp.abs(x)))


def _sigmoid(x):
    return 1.0 / (1.0 + jnp.exp(-x))


def _proj_kernel(x_ref, g_ref, w_ref, ws_ref, o_ref, os_ref, xn_ref):
    @pl.when(pl.program_id(1) == 0)
    def _():
        xn = _rms_rows(x_ref[...], g_ref[...], NORM_EPS).astype(BF16)
        xn_ref[...] = xn
        os_ref[...] = jnp.dot(xn, ws_ref[...], preferred_element_type=F32)

    o_ref[...] = jnp.dot(xn_ref[...], w_ref[...], preferred_element_type=F32).astype(o_ref.dtype)


def _in_proj(h, g, w_main, w_side):
    n = w_main.shape[1]
    grid = (TOKENS // PROJ_TM, n // PROJ_TN)
    return pl.pallas_call(
        _proj_kernel,
        grid=grid,
        in_specs=[
            pl.BlockSpec((PROJ_TM, D_MODEL), lambda i, j: (i, 0)),
            pl.BlockSpec((1, D_MODEL), lambda i, j: (0, 0)),
            pl.BlockSpec((D_MODEL, PROJ_TN), lambda i, j: (0, j)),
            pl.BlockSpec((D_MODEL, LANES), lambda i, j: (0, 0)),
        ],
        out_specs=[
            pl.BlockSpec((PROJ_TM, PROJ_TN), lambda i, j: (i, j)),
            pl.BlockSpec((PROJ_TM, LANES), lambda i, j: (i, 0)),
        ],
        out_shape=[
            jax.ShapeDtypeStruct((TOKENS, n), BF16),
            jax.ShapeDtypeStruct((TOKENS, LANES), F32),
        ],
        scratch_shapes=[pltpu.VMEM((PROJ_TM, D_MODEL), BF16)],
        compiler_params=_params("arbitrary", "arbitrary"),
        name="in_proj",
    )(h, g, w_main, w_side)


def _same_chunk(r, c, chunk):
    return jnp.bitwise_xor(r, c) < chunk


def _tri_ones(n, chunk):
    r = lax.broadcasted_iota(jnp.int32, (n, n), 0)
    c = lax.broadcasted_iota(jnp.int32, (n, n), 1)
    return jnp.where((c <= r) & _same_chunk(r, c, chunk), 1.0, 0.0).astype(F32)


def _fox_gate_kernel(s_ref, bf_ref, c_ref, carry_ref):
    @pl.when(pl.program_id(1) == 0)
    def _():
        carry_ref[...] = jnp.zeros_like(carry_ref)

    x = s_ref[...] + bf_ref[...]
    log_f = jnp.minimum(x, 0.0) - jnp.log1p(jnp.exp(-jnp.abs(x)))
    c = _dot_f32(_tri_ones(GATE_TM, GATE_TM), log_f) + carry_ref[...]
    c_ref[...] = c
    carry_ref[...] = c[GATE_TM - 1:GATE_TM, :]


def _fox_gates(side, b_f_lanes):
    return pl.pallas_call(
        _fox_gate_kernel,
        grid=(BATCH, SEQ // GATE_TM),
        in_specs=[
            pl.BlockSpec((GATE_TM, LANES), lambda b, t: (b * (SEQ // GATE_TM) + t, 0)),
            pl.BlockSpec((1, LANES), lambda b, t: (0, 0)),
        ],
        out_specs=pl.BlockSpec((GATE_TM, LANES), lambda b, t: (b * (SEQ // GATE_TM) + t, 0)),
        out_shape=jax.ShapeDtypeStruct((TOKENS, LANES), F32),
        scratch_shapes=[pltpu.VMEM((1, LANES), F32)],
        compiler_params=_params("arbitrary", "arbitrary"),
        name="fox_gates",
    )(side, b_f_lanes)


def _gdn_gate_kernel(s_ref, alog_ref, dtb_ref, o_ref):
    x = s_ref[...]
    log_decay = -jnp.exp(alog_ref[...]) * _softplus(x + dtb_ref[...])
    g = _dot_f32(_tri_ones(GATE_TM, GDN_CHUNK), log_decay)
    lane = lax.broadcasted_iota(jnp.int32, x.shape, 1)
    o_ref[...] = jnp.where(lane < GDN_HEADS, g, _sigmoid(x))


def _gdn_gates(side, a_log_lanes, dt_bias_lanes):
    return pl.pallas_call(
        _gdn_gate_kernel,
        grid=(TOKENS // GATE_TM,),
        in_specs=[
            pl.BlockSpec((GATE_TM, LANES), lambda t: (t, 0)),
            pl.BlockSpec((1, LANES), lambda t: (0, 0)),
            pl.BlockSpec((1, LANES), lambda t: (0, 0)),
        ],
        out_specs=pl.BlockSpec((GATE_TM, LANES), lambda t: (t, 0)),
        out_shape=jax.ShapeDtypeStruct((TOKENS, LANES), F32),
        compiler_params=_params("arbitrary"),
        name="gdn_gates",
    )(side, a_log_lanes, dt_bias_lanes)


def _lane_pick(tile, lane_idx):
    lane = lax.broadcasted_iota(jnp.int32, tile.shape, 1)
    return jnp.sum(jnp.where(lane == lane_idx, tile, 0.0), axis=1, keepdims=True)


def _flash_static(q_of, k_ref, v_ref, cq_of, ck_of, emit, tq, tk, qk_scale):
    for i in range(SEQ // tq):
        q = q_of(i)
        rows = q.shape[0]
        cq = cq_of(i)
        m = l = acc = None
        for j in range(((i + 1) * tq + tk - 1) // tk):
            k = k_ref[j * tk:(j + 1) * tk, :]
            v = v_ref[j * tk:(j + 1) * tk, :]
            a = lax.dot_general(q, k, NT_DIMS, preferred_element_type=F32) * qk_scale - ck_of(j)
            if (j + 1) * tk - 1 > i * tq:
                qpos = i * tq + (lax.broadcasted_iota(jnp.int32, (rows, tk), 0) & (tq - 1))
                kpos = j * tk + lax.broadcasted_iota(jnp.int32, (rows, tk), 1)
                a = jnp.where(qpos >= kpos, a, -jnp.inf)
            m_blk = jnp.max(a, axis=1, keepdims=True) + cq
            if j == 0:
                m = m_blk
                p = jnp.exp2(a - (m - cq))
                l = jnp.sum(p, axis=1, keepdims=True)
                acc = jnp.dot(p.astype(BF16), v, preferred_element_type=F32)
            else:
                m_new = jnp.maximum(m, m_blk)
                alpha = jnp.exp2(m - m_new)
                p = jnp.exp2(a - (m_new - cq))
                l = alpha * l + jnp.sum(p, axis=1, keepdims=True)
                acc = alpha * acc + jnp.dot(p.astype(BF16), v, preferred_element_type=F32)
                m = m_new
        emit(i, acc / l)


def _fox_attn_kernel(q_ref, k_ref, v_ref, ctm_ref, chm_ref, o_ref):
    h = pl.program_id(1)

    def q_of(i):
        return q_ref[i * FOX_TQ:(i + 1) * FOX_TQ, :]

    def cq_of(i):
        return _lane_pick(ctm_ref[i * FOX_TQ:(i + 1) * FOX_TQ, :], h) * LOG2E

    def ck_of(j):
        return chm_ref[h, j:j + 1, :] * LOG2E

    def emit(i, o):
        o_ref[i * FOX_TQ:(i + 1) * FOX_TQ, :] = o.astype(o_ref.dtype)

    _flash_static(q_of, k_ref, v_ref, cq_of, ck_of, emit, FOX_TQ, FOX_TK,
                  FOX_HEAD_DIM ** -0.5 * LOG2E)


def _fox_attention(qkv, c_tm, c_hm):
    return pl.pallas_call(
        _fox_attn_kernel,
        grid=(BATCH, FOX_HEADS),
        in_specs=[
            pl.BlockSpec((SEQ, LANES), lambda b, h: (b, h)),
            pl.BlockSpec((SEQ, LANES), lambda b, h: (b, FOX_K_BLK + h)),
            pl.BlockSpec((SEQ, LANES), lambda b, h: (b, FOX_V_BLK + h)),
            pl.BlockSpec((SEQ, LANES), lambda b, h: (b, 0)),
            pl.BlockSpec((FOX_HEADS, SEQ // FOX_TK, FOX_TK), lambda b, h: (0, b, 0)),
        ],
        out_specs=pl.BlockSpec((SEQ, LANES), lambda b, h: (b, h)),
        out_shape=jax.ShapeDtypeStruct((TOKENS, FOX_HEADS * FOX_HEAD_DIM), BF16),
        compiler_params=_params("arbitrary", "arbitrary"),
        name="fox_attention",
    )(qkv, qkv, qkv, c_tm, c_hm)


def _da_attn_kernel(slope_ref, q_ref, k_ref, v_ref, lam_ref, norm_ref, o_ref, *, lam_init):
    h = pl.program_id(1)
    c_per_pos = -slope_ref[h] * LOG2E

    lv = lam_ref[...]
    lam = (jnp.exp(jnp.sum(lv[0:1] * lv[1:2], axis=1, keepdims=True))
           - jnp.exp(jnp.sum(lv[2:3] * lv[3:4], axis=1, keepdims=True)) + lam_init)

    def q_of(i):
        q = q_ref[i * DA_TQ:(i + 1) * DA_TQ, :]
        lane = lax.broadcasted_iota(jnp.int32, q.shape, 1)
        zero = jnp.zeros_like(q)
        return jnp.concatenate([jnp.where(lane < DA_QK_DIM, q, zero),
                                jnp.where(lane >= DA_QK_DIM, q, zero)], axis=0)

    def cq_of(i):
        pos = i * DA_TQ + (lax.broadcasted_iota(jnp.int32, (2 * DA_TQ, 1), 0) & (DA_TQ - 1))
        return c_per_pos * pos.astype(F32)

    def ck_of(j):
        pos = j * DA_TK + lax.broadcasted_iota(jnp.int32, (1, DA_TK), 1)
        return c_per_pos * pos.astype(F32)

    def emit(i, o2):
        o = o2[:DA_TQ] - lam * o2[DA_TQ:]
        o = _rms_rows(o, norm_ref[...], NORM_EPS) * (1.0 - lam_init)
        o_ref[i * DA_TQ:(i + 1) * DA_TQ, :] = o.astype(o_ref.dtype)

    _flash_static(q_of, k_ref, v_ref, cq_of, ck_of, emit, DA_TQ, DA_TK, DA_QK_DIM ** -0.5 * LOG2E)


def _da_attention(proj, slopes, lam_vecs, da_norm, lam_init):
    return pl.pallas_call(
        functools.partial(_da_attn_kernel, lam_init=lam_init),
        grid=(BATCH, DA_HEADS),
        in_specs=[
            pl.BlockSpec(memory_space=pltpu.SMEM),
            pl.BlockSpec((SEQ, LANES), lambda b, h: (b, DA_Q_BLK + h)),
            pl.BlockSpec((SEQ, LANES), lambda b, h: (b, DA_K_BLK + h)),
            pl.BlockSpec((SEQ, LANES), lambda b, h: (b, DA_V_BLK + h)),
            pl.BlockSpec((4, DA_QK_DIM), lambda b, h: (0, 0)),
            pl.BlockSpec((1, DA_V_DIM), lambda b, h: (0, 0)),
        ],
        out_specs=pl.BlockSpec((SEQ, LANES), lambda b, h: (b, h)),
        out_shape=jax.ShapeDtypeStruct((TOKENS, DA_HEADS * DA_V_DIM), BF16),
        compiler_params=_params("arbitrary", "arbitrary"),
        name="da_attention",
    )(slopes, proj, proj, proj, lam_vecs, da_norm)


def _conv_silu(x_ref, w_ref):
    x = x_ref[...].astype(F32)
    w = w_ref[...]
    row = lax.broadcasted_iota(jnp.int32, x.shape, 0)
    y = x * w[CONV_WIDTH - 1:CONV_WIDTH]
    for back in range(1, CONV_WIDTH):
        shifted = jnp.where(row >= back, pltpu.roll(x, back, axis=0), 0.0)
        y = y + shifted * w[CONV_WIDTH - 1 - back:CONV_WIDTH - back]
    return y * _sigmoid(y)


def _l2norm_rows(t):
    return t * lax.rsqrt(jnp.sum(t * t, axis=-1, keepdims=True) + L2_EPS)


def _gdn_kernel(q_ref, k_ref, v_ref, z_ref, wq_ref, wk_ref, wv_ref, gtm_ref, ghm_ref, norm_ref,
                o_ref, qs_ref, ks_ref, vs_ref, qp_ref, o0_ref, mt_ref, nt_ref, egl_ref, os_ref):
    h = pl.program_id(1)
    c = GDN_CHUNK
    grp = GDN_GROUP
    qs_ref[...] = _l2norm_rows(_conv_silu(q_ref, wq_ref)) * (GDN_K_DIM ** -0.5)
    ks_ref[...] = _l2norm_rows(_conv_silu(k_ref, wk_ref))
    vs_ref[...] = _conv_silu(v_ref, wv_ref)

    ii = lax.broadcasted_iota(jnp.int32, (grp, grp), 0)
    jj = lax.broadcasted_iota(jnp.int32, (grp, grp), 1)
    sib = jnp.bitwise_xor(ii, jj)
    same = sib < c
    incl = same & (ii >= jj)
    strict = same & (ii > jj)
    last_of_chunk = jj == (ii | (c - 1))
    eye = jnp.where(ii == jj, 1.0, 0.0).astype(F32)

    def local_part(gi):
        r0 = pl.multiple_of(gi * grp, grp)
        q = qs_ref[pl.ds(r0, grp), :]
        k = ks_ref[pl.ds(r0, grp), :]
        v = vs_ref[pl.ds(r0, grp), :]
        gates = gtm_ref[pl.ds(r0, grp), :]
        g_col = _lane_pick(gates, h)
        beta = _lane_pick(gates, GDN_HEADS + h)
        g_row = ghm_ref[h, pl.ds(gi, 1), :]
        gl_col = jnp.sum(jnp.where(last_of_chunk, g_row, 0.0), axis=1, keepdims=True)

        decay = jnp.exp(jnp.where(incl, g_col - g_row, -jnp.inf))
        eg = jnp.exp(g_col)
        k_beta = k * beta
        a = jnp.where(strict, _dot_bf16(k_beta, k, NT_DIMS) * decay, 0.0)
        t_inv = eye - jnp.where(sib == 1, a, 0.0)
        s = 2
        while s < c:
            a_s = jnp.where((sib >= s) & (sib < 2 * s), a, 0.0)
            t_inv = t_inv - _dot_bf16(t_inv, _dot_bf16(a_s, t_inv))
            s *= 2
        wu = _dot_bf16(t_inv, jnp.concatenate([k_beta * eg, v * beta], axis=1))
        intra = _dot_bf16(q, k, NT_DIMS) * decay
        iwu = _dot_bf16(intra, wu)
        qp_ref[pl.ds(r0, grp), :] = (q * eg - iwu[:, :GDN_K_DIM]).astype(BF16)
        o0_ref[pl.ds(r0, grp), :] = iwu[:, GDN_K_DIM:]
        k_dec = k * jnp.exp(gl_col - g_col)
        for ci in range(CHUNKS_PER_GROUP):
            n = gi * CHUNKS_PER_GROUP + ci
            zt = _dot_bf16(wu[ci * c:(ci + 1) * c, :], k_dec[ci * c:(ci + 1) * c, :], TN_DIMS)
            m0 = pl.multiple_of(n * GDN_K_DIM, GDN_K_DIM)
            mt_ref[pl.ds(m0, GDN_K_DIM), :] = zt[:GDN_K_DIM].astype(BF16)
            nt_ref[pl.ds(m0, GDN_V_DIM), :] = zt[GDN_K_DIM:]
            gl = g_row[:, (ci + 1) * c - 1:(ci + 1) * c]
            egl_ref[pl.ds(n, 1), :] = jnp.broadcast_to(jnp.exp(gl), (1, LANES))

    def local_pair(gp, carry):
        local_part(2 * gp)
        local_part(2 * gp + 1)
        return carry

    lax.fori_loop(0, N_GROUPS // 2, local_pair, 0)

    def step(n, st):
        r0 = pl.multiple_of(n * c, c)
        m0 = pl.multiple_of(n * GDN_K_DIM, GDN_K_DIM)
        st_b = st.astype(BF16)
        os_ref[pl.ds(r0, c), :] = (
            lax.dot_general(qp_ref[pl.ds(r0, c), :], st_b, NT_DIMS, preferred_element_type=F32)
            + o0_ref[pl.ds(r0, c), :])
        return (st * egl_ref[pl.ds(n, 1), :]
                - jnp.dot(st_b, mt_ref[pl.ds(m0, GDN_K_DIM), :], preferred_element_type=F32)
                + nt_ref[pl.ds(m0, GDN_V_DIM), :])

    lax.fori_loop(0, N_CHUNKS, step, jnp.zeros((GDN_V_DIM, GDN_K_DIM), F32), unroll=2)

    z = z_ref[...].astype(F32)
    o = _rms_rows(os_ref[...], norm_ref[...], NORM_EPS) * (z * _sigmoid(z))
    o_ref[...] = o.astype(o_ref.dtype)


def _gdn(proj, conv_w, gates_tm, g_hm, gdn_norm):
    seq_blk = lambda off: pl.BlockSpec((SEQ, LANES), lambda b, h: (b, off + h))
    conv_blk = lambda off: pl.BlockSpec((CONV_WIDTH, LANES), lambda b, h: (0, off + h))
    return pl.pallas_call(
        _gdn_kernel,
        grid=(BATCH, GDN_HEADS),
        in_specs=[
            seq_blk(GDN_Q_BLK), seq_blk(GDN_K_BLK), seq_blk(GDN_V_BLK), seq_blk(GDN_Z_BLK),
            conv_blk(0), conv_blk(GDN_HEADS), conv_blk(2 * GDN_HEADS),
            pl.BlockSpec((SEQ, LANES), lambda b, h: (b, 0)),
            pl.BlockSpec((GDN_HEADS, N_GROUPS, GDN_GROUP), lambda b, h: (0, b, 0)),
            pl.BlockSpec((1, GDN_V_DIM), lambda b, h: (0, 0)),
        ],
        out_specs=pl.BlockSpec((SEQ, LANES), lambda b, h: (b, h)),
        out_shape=jax.ShapeDtypeStruct((TOKENS, GDN_HEADS * GDN_V_DIM), BF16),
        scratch_shapes=[
            pltpu.VMEM((SEQ, LANES), F32),
            pltpu.VMEM((SEQ, LANES), F32),
            pltpu.VMEM((SEQ, LANES), F32),
            pltpu.VMEM((SEQ, LANES), BF16),
            pltpu.VMEM((SEQ, LANES), F32),
            pltpu.VMEM((N_CHUNKS * GDN_K_DIM, LANES), BF16),
            pltpu.VMEM((N_CHUNKS * GDN_V_DIM, LANES), F32),
            pltpu.VMEM((N_CHUNKS, LANES), F32),
            pltpu.VMEM((SEQ, LANES), F32),
        ],
        compiler_params=_params("arbitrary", "arbitrary"),
        name="gdn",
    )(proj, proj, proj, proj, conv_w, conv_w, conv_w, gates_tm, g_hm, gdn_norm)


def _out_proj_kernel(xa_ref, xb_ref, wa_ref, wb_ref, h_ref, o_ref):
    o_ref[...] = (h_ref[...]
                  + jnp.dot(xa_ref[...], wa_ref[...], preferred_element_type=F32)
                  + jnp.dot(xb_ref[...], wb_ref[...], preferred_element_type=F32))


def _out_proj(xa, xb_spec_col, xb, w_out, h):
    half = w_out.shape[0] // 2
    return pl.pallas_call(
        _out_proj_kernel,
        grid=(TOKENS // OUT_TM, D_MODEL // OUT_TN),
        in_specs=[
            pl.BlockSpec((OUT_TM, half), lambda i, j: (i, 0)),
            pl.BlockSpec((OUT_TM, half), lambda i, j: (i, xb_spec_col)),
            pl.BlockSpec((half, OUT_TN), lambda i, j: (0, j)),
            pl.BlockSpec((half, OUT_TN), lambda i, j: (1, j)),
            pl.BlockSpec((OUT_TM, OUT_TN), lambda i, j: (i, j)),
        ],
        out_specs=pl.BlockSpec((OUT_TM, OUT_TN), lambda i, j: (i, j)),
        out_shape=jax.ShapeDtypeStruct((TOKENS, D_MODEL), F32),
        compiler_params=_params("arbitrary", "arbitrary"),
        name="out_proj",
    )(xa, xb, w_out, w_out, h)


def _mlp_kernel(h_ref, g_ref, wu_ref, wd_ref, gf_ref, o_ref, xn_ref, *, final_norm):
    f = pl.program_id(1)

    @pl.when(f == 0)
    def _():
        x = h_ref[...]
        xn_ref[...] = _rms_rows(x, g_ref[...], NORM_EPS).astype(BF16)
        o_ref[...] = x

    a = jnp.maximum(jnp.dot(xn_ref[...], wu_ref[...], preferred_element_type=F32), 0.0)
    o_ref[...] += jnp.dot((a * a).astype(BF16), wd_ref[...], preferred_element_type=F32)

    if final_norm:
        @pl.when(f == pl.num_programs(1) - 1)
        def _():
            o_ref[...] = _rms_rows(o_ref[...], gf_ref[...], NORM_EPS)


def _mlp(h, g, w_up, w_down, g_final, final_norm):
    return pl.pallas_call(
        functools.partial(_mlp_kernel, final_norm=final_norm),
        grid=(TOKENS // MLP_TM, D_FF // MLP_TF),
        in_specs=[
            pl.BlockSpec((MLP_TM, D_MODEL), lambda i, f: (i, 0)),
            pl.BlockSpec((1, D_MODEL), lambda i, f: (0, 0)),
            pl.BlockSpec((D_MODEL, MLP_TF), lambda i, f: (0, f)),
            pl.BlockSpec((MLP_TF, D_MODEL), lambda i, f: (f, 0)),
            pl.BlockSpec((1, D_MODEL), lambda i, f: (0, 0)),
        ],
        out_specs=pl.BlockSpec((MLP_TM, D_MODEL), lambda i, f: (i, 0)),
        out_shape=jax.ShapeDtypeStruct((TOKENS, D_MODEL), F32),
        scratch_shapes=[pltpu.VMEM((MLP_TM, D_MODEL), BF16)],
        compiler_params=_params("arbitrary", "arbitrary"),
        name="mlp",
    )(h, g, w_up, w_down, g_final)


def _lanes(vec, offset=0):
    return jnp.zeros((1, LANES), F32).at[0, offset:offset + vec.shape[0]].set(vec.astype(F32))


def _side_weight(w):
    return jnp.pad(w, ((0, 0), (0, LANES - w.shape[1]))).astype(BF16)


def _even_mixer(h, g_mix, w_in, conv_w, lam_vecs, da_norm, a_log, dt_bias, gdn_norm, w_out, layer):
    proj, side = _in_proj(h, g_mix, w_in[:, :EVEN_MAIN].astype(BF16), _side_weight(w_in[:, EVEN_MAIN:]))
    lam_init = 0.8 - 0.6 * math.exp(-0.3 * layer)
    slopes = 2.0 ** (-8.0 * jnp.arange(1, DA_HEADS + 1, dtype=F32) / DA_HEADS)
    o_a = _da_attention(proj, slopes, lam_vecs, da_norm.reshape(1, DA_V_DIM), lam_init)

    gates_tm = _gdn_gates(side, _lanes(a_log), _lanes(dt_bias))
    g_hm = gates_tm[:, :GDN_HEADS].T.reshape(GDN_HEADS, BATCH * N_GROUPS, GDN_GROUP)
    o_b = _gdn(proj, conv_w, gates_tm, g_hm, gdn_norm.reshape(1, GDN_V_DIM))
    return _out_proj(o_a, 0, o_b, w_out.astype(BF16), h)


def _odd_mixer(h, g_mix, w_in, b_f, w_out):
    qkv, side = _in_proj(h, g_mix, w_in[:, :ODD_MAIN].astype(BF16), _side_weight(w_in[:, ODD_MAIN:]))
    c_tm = _fox_gates(side, _lanes(b_f))
    c_hm = c_tm[:, :FOX_HEADS].T.reshape(FOX_HEADS, BATCH * (SEQ // FOX_TK), FOX_TK)
    o = _fox_attention(qkv, c_tm, c_hm)
    return _out_proj(o, 1, o, w_out.astype(BF16), h)


def kernel(x, norm_mix, norm_mlp, norm_final, w_in_even, conv_w, lam_q1, lam_k1, lam_q2, lam_k2,
           da_norm, gdn_a_log, gdn_dt_bias, gdn_norm, w_out_even, w_in_odd, fox_b_f, w_out_odd,
           w_up, w_down):
    h = x.reshape(TOKENS, D_MODEL)
    g_final = norm_final.reshape(1, D_MODEL)
    for layer in range(DEPTH):
        i = layer // 2
        g_mix = norm_mix[layer].reshape(1, D_MODEL)
        if layer % 2 == 0:
            lam_vecs = jnp.stack([lam_q1[i], lam_k1[i], lam_q2[i], lam_k2[i]]).astype(F32)
            h = _even_mixer(h, g_mix, w_in_even[i], conv_w[i], lam_vecs, da_norm[i], gdn_a_log[i],
                            gdn_dt_bias[i], gdn_norm[i], w_out_even[i], layer)
        else:
            h = _odd_mixer(h, g_mix, w_in_odd[i], fox_b_f[i], w_out_odd[i])
        h = _mlp(h, norm_mlp[layer].reshape(1, D_MODEL), w_up[layer].astype(BF16),
                 w_down[layer].astype(BF16), g_final, layer == DEPTH - 1)
    return h.reshape(BATCH, SEQ, D_MODEL)
```

```python
import functools
import math

import jax
import jax.numpy as jnp
from jax import lax
from jax.experimental import pallas as pl
from jax.experimental.pallas import tpu as pltpu

F32 = jnp.float32
BF16 = jnp.bfloat16

D_MODEL = 2048
BATCH = 4
SEQ = 2048
DEPTH = 4
TOKENS = BATCH * SEQ

DA_HEADS = 8
DA_QK_DIM = 64
DA_V_DIM = 128
GDN_HEADS = 8
GDN_K_DIM = 128
GDN_V_DIM = 128
GDN_CHUNK = 64
CONV_WIDTH = 4
FOX_HEADS = 16
FOX_HEAD_DIM = 128
D_FF = 4 * D_MODEL
NORM_EPS = 1e-6
L2_EPS = 1e-6

LANES = 128
VMEM_LIMIT = 56 * 1024 * 1024

EVEN_MAIN = 3 * 1024 + 3 * 1024 + 1024
DA_Q_BLK, DA_K_BLK, DA_V_BLK = 0, 8, 16
GDN_Q_BLK, GDN_K_BLK, GDN_V_BLK, GDN_Z_BLK = 24, 32, 40, 48
ODD_MAIN = 3 * FOX_HEADS * FOX_HEAD_DIM
FOX_K_BLK, FOX_V_BLK = 16, 32

PROJ_TM, PROJ_TN = 1024, 512
MLP_TM, MLP_TF = 1024, 512
OUT_TM, OUT_TN = 1024, 512
GATE_TM = 256
FOX_TQ, FOX_TK_WIDE = 256, 256
DA_TQ, DA_TK_WIDE = 256, 256
N_CHUNKS = SEQ // GDN_CHUNK
GDN_GROUP = 256
CHUNKS_PER_GROUP = GDN_GROUP // GDN_CHUNK
N_GROUPS = SEQ // GDN_GROUP
GDN_INTERLEAVE = 4

LOG2E = math.log2(math.e)

NT_DIMS = (((1,), (1,)), ((), ()))
TN_DIMS = (((0,), (0,)), ((), ()))


def _params(*sem):
    return pltpu.CompilerParams(dimension_semantics=sem, vmem_limit_bytes=VMEM_LIMIT)


def _rms_rows(x, g, eps):
    ms = jnp.mean(x * x, axis=-1, keepdims=True)
    return x * lax.rsqrt(ms + eps) * g


def _dot_bf16(a, b, dims=None):
    a = a.astype(BF16)
    b = b.astype(BF16)
    if dims is None:
        return jnp.dot(a, b, preferred_element_type=F32)
    return lax.dot_general(a, b, dims, preferred_element_type=F32)


def _dot_f32(a, b):
    return jnp.dot(a, b, precision=lax.Precision.HIGHEST, preferred_element_type=F32)


def _softplus(x):
    return jnp.maximum(x, 0.0) + jnp.log1p(jnp.exp(-jnp.abs(x)))


def _sigmoid(x):
    return 1.0 / (1.0 + jnp.exp(-x))


def _proj_kernel(x_ref, g_ref, w_ref, ws_ref, o_ref, os_ref, xn_ref):
    @pl.when(pl.program_id(1) == 0)
    def _():
        xn = _rms_rows(x_ref[...], g_ref[...], NORM_EPS).astype(BF16)
        xn_ref[...] = xn
        os_ref[...] = jnp.dot(xn, ws_ref[...], preferred_element_type=F32)

    w = w_ref[...].astype(BF16)
    o_ref[...] = jnp.dot(xn_ref[...], w, preferred_element_type=F32).astype(o_ref.dtype)


def _in_proj(h, g, w_stack, idx, n, w_side):
    grid = (TOKENS // PROJ_TM, n // PROJ_TN)
    return pl.pallas_call(
        _proj_kernel,
        grid=grid,
        in_specs=[
            pl.BlockSpec((PROJ_TM, D_MODEL), lambda i, j: (i, 0)),
            pl.BlockSpec((1, D_MODEL), lambda i, j: (0, 0)),
            pl.BlockSpec((None, D_MODEL, PROJ_TN), lambda i, j: (idx, 0, j)),
            pl.BlockSpec((D_MODEL, LANES), lambda i, j: (0, 0)),
        ],
        out_specs=[
            pl.BlockSpec((PROJ_TM, PROJ_TN), lambda i, j: (i, j)),
            pl.BlockSpec((PROJ_TM, LANES), lambda i, j: (i, 0)),
        ],
        out_shape=[
            jax.ShapeDtypeStruct((TOKENS, n), BF16),
            jax.ShapeDtypeStruct((TOKENS, LANES), F32),
        ],
        scratch_shapes=[pltpu.VMEM((PROJ_TM, D_MODEL), BF16)],
        compiler_params=_params("arbitrary", "arbitrary"),
        name="in_proj",
    )(h, g, w_stack, w_side)


def _same_chunk(r, c, chunk):
    return jnp.bitwise_xor(r, c) < chunk


def _tri_ones(n, chunk):
    r = lax.broadcasted_iota(jnp.int32, (n, n), 0)
    c = lax.broadcasted_iota(jnp.int32, (n, n), 1)
    return jnp.where((c <= r) & _same_chunk(r, c, chunk), 1.0, 0.0).astype(F32)


def _fox_gate_kernel(s_ref, bf_ref, c_ref, carry_ref):
    @pl.when(pl.program_id(1) == 0)
    def _():
        carry_ref[...] = jnp.zeros_like(carry_ref)

    x = s_ref[...] + bf_ref[...]
    log_f = jnp.minimum(x, 0.0) - jnp.log1p(jnp.exp(-jnp.abs(x)))
    c = _dot_f32(_tri_ones(GATE_TM, GATE_TM), log_f) + carry_ref[...]
    c_ref[...] = c
    carry_ref[...] = c[GATE_TM - 1:GATE_TM, :]


def _fox_gates(side, b_f_lanes):
    return pl.pallas_call(
        _fox_gate_kernel,
        grid=(BATCH, SEQ // GATE_TM),
        in_specs=[
            pl.BlockSpec((GATE_TM, LANES), lambda b, t: (b * (SEQ // GATE_TM) + t, 0)),
            pl.BlockSpec((1, LANES), lambda b, t: (0, 0)),
        ],
        out_specs=pl.BlockSpec((GATE_TM, LANES), lambda b, t: (b * (SEQ // GATE_TM) + t, 0)),
        out_shape=jax.ShapeDtypeStruct((TOKENS, LANES), F32),
        scratch_shapes=[pltpu.VMEM((1, LANES), F32)],
        compiler_params=_params("arbitrary", "arbitrary"),
        name="fox_gates",
    )(side, b_f_lanes)


def _gdn_gate_kernel(s_ref, alog_ref, dtb_ref, o_ref):
    x = s_ref[...]
    log_decay = -jnp.exp(alog_ref[...]) * _softplus(x + dtb_ref[...])
    g = _dot_f32(_tri_ones(GATE_TM, GDN_CHUNK), log_decay)
    lane = lax.broadcasted_iota(jnp.int32, x.shape, 1)
    o_ref[...] = jnp.where(lane < GDN_HEADS, g, _sigmoid(x))


def _gdn_gates(side, a_log_lanes, dt_bias_lanes):
    return pl.pallas_call(
        _gdn_gate_kernel,
        grid=(TOKENS // GATE_TM,),
        in_specs=[
            pl.BlockSpec((GATE_TM, LANES), lambda t: (t, 0)),
            pl.BlockSpec((1, LANES), lambda t: (0, 0)),
            pl.BlockSpec((1, LANES), lambda t: (0, 0)),
        ],
        out_specs=pl.BlockSpec((GATE_TM, LANES), lambda t: (t, 0)),
        out_shape=jax.ShapeDtypeStruct((TOKENS, LANES), F32),
        compiler_params=_params("arbitrary"),
        name="gdn_gates",
    )(side, a_log_lanes, dt_bias_lanes)


def _lane_pick(tile, lane_idx):
    lane = lax.broadcasted_iota(jnp.int32, tile.shape, 1)
    return jnp.sum(jnp.where(lane == lane_idx, tile, 0.0), axis=1, keepdims=True)


def _flash_static(q_of, k_ref, v_ref, cq_of, ck_of, emit, tq, tk_wide, qk_scale):
    for i in range(SEQ // tq):
        q = q_of(i)
        rows = q.shape[0]
        cq = cq_of(i)
        blocks, pos = [], 0
        while pos < i * tq:
            width = tk_wide if pos + tk_wide <= i * tq else tq
            blocks.append((pos, width, False))
            pos += width
        blocks.append((i * tq, tq, True))
        m = l = acc = None
        for j, (k0, width, diagonal) in enumerate(blocks):
            k = k_ref[k0:k0 + width, :]
            v = v_ref[k0:k0 + width, :]
            a = lax.dot_general(q, k, NT_DIMS, preferred_element_type=F32) * qk_scale - ck_of(k0, width)
            if diagonal:
                qpos = lax.broadcasted_iota(jnp.int32, (rows, width), 0) & (tq - 1)
                kpos = lax.broadcasted_iota(jnp.int32, (rows, width), 1)
                a = jnp.where(qpos >= kpos, a, -jnp.inf)
            m_blk = jnp.max(a, axis=1, keepdims=True) + cq
            if j == 0:
                m = m_blk
                p = jnp.exp2(a - (m - cq))
                l = jnp.sum(p, axis=1, keepdims=True)
                acc = jnp.dot(p.astype(BF16), v, preferred_element_type=F32)
            else:
                m_new = jnp.maximum(m, m_blk)
                alpha = jnp.exp2(m - m_new)
                p = jnp.exp2(a - (m_new - cq))
                l = alpha * l + jnp.sum(p, axis=1, keepdims=True)
                acc = alpha * acc + jnp.dot(p.astype(BF16), v, preferred_element_type=F32)
                m = m_new
        emit(i, acc / l)


def _fox_attn_kernel(q_ref, k_ref, v_ref, ctm_ref, chm_ref, o_ref):
    h = pl.program_id(1)

    def q_of(i):
        return q_ref[i * FOX_TQ:(i + 1) * FOX_TQ, :]

    def cq_of(i):
        return _lane_pick(ctm_ref[i * FOX_TQ:(i + 1) * FOX_TQ, :], h) * LOG2E

    def ck_of(k0, width):
        return chm_ref[pl.ds(h, 1), k0:k0 + width] * LOG2E

    def emit(i, o):
        o_ref[i * FOX_TQ:(i + 1) * FOX_TQ, :] = o.astype(o_ref.dtype)

    _flash_static(q_of, k_ref, v_ref, cq_of, ck_of, emit, FOX_TQ, FOX_TK_WIDE,
                  FOX_HEAD_DIM ** -0.5 * LOG2E)


def _fox_attention(qkv, c_tm, c_hm):
    return pl.pallas_call(
        _fox_attn_kernel,
        grid=(BATCH, FOX_HEADS),
        in_specs=[
            pl.BlockSpec((SEQ, LANES), lambda b, h: (b, h)),
            pl.BlockSpec((SEQ, LANES), lambda b, h: (b, FOX_K_BLK + h)),
            pl.BlockSpec((SEQ, LANES), lambda b, h: (b, FOX_V_BLK + h)),
            pl.BlockSpec((SEQ, LANES), lambda b, h: (b, 0)),
            pl.BlockSpec((None, FOX_HEADS, SEQ), lambda b, h: (b, 0, 0)),
        ],
        out_specs=pl.BlockSpec((SEQ, LANES), lambda b, h: (b, h)),
        out_shape=jax.ShapeDtypeStruct((TOKENS, FOX_HEADS * FOX_HEAD_DIM), BF16),
        compiler_params=_params("arbitrary", "arbitrary"),
        name="fox_attention",
    )(qkv, qkv, qkv, c_tm, c_hm)


def _da_attn_kernel(slope_ref, q_ref, k_ref, v_ref, lam_ref, norm_ref, o_ref, *, lam_init):
    h = pl.program_id(1)
    c_per_pos = -slope_ref[h] * LOG2E

    lv = lam_ref[...]
    lam = (jnp.exp(jnp.sum(lv[0:1] * lv[1:2], axis=1, keepdims=True))
           - jnp.exp(jnp.sum(lv[2:3] * lv[3:4], axis=1, keepdims=True)) + lam_init)

    def q_of(i):
        q = q_ref[i * DA_TQ:(i + 1) * DA_TQ, :]
        lane = lax.broadcasted_iota(jnp.int32, q.shape, 1)
        zero = jnp.zeros_like(q)
        return jnp.concatenate([jnp.where(lane < DA_QK_DIM, q, zero),
                                jnp.where(lane >= DA_QK_DIM, q, zero)], axis=0)

    def cq_of(i):
        pos = i * DA_TQ + (lax.broadcasted_iota(jnp.int32, (2 * DA_TQ, 1), 0) & (DA_TQ - 1))
        return c_per_pos * pos.astype(F32)

    def ck_of(k0, width):
        pos = k0 + lax.broadcasted_iota(jnp.int32, (1, width), 1)
        return c_per_pos * pos.astype(F32)

    def emit(i, o2):
        o = o2[:DA_TQ] - lam * o2[DA_TQ:]
        o = _rms_rows(o, norm_ref[...], NORM_EPS) * (1.0 - lam_init)
        o_ref[i * DA_TQ:(i + 1) * DA_TQ, :] = o.astype(o_ref.dtype)

    _flash_static(q_of, k_ref, v_ref, cq_of, ck_of, emit, DA_TQ, DA_TK_WIDE, DA_QK_DIM ** -0.5 * LOG2E)


def _da_attention(proj, slopes, lam_vecs, da_norm, lam_init):
    return pl.pallas_call(
        functools.partial(_da_attn_kernel, lam_init=lam_init),
        grid=(BATCH, DA_HEADS),
        in_specs=[
            pl.BlockSpec(memory_space=pltpu.SMEM),
            pl.BlockSpec((SEQ, LANES), lambda b, h: (b, DA_Q_BLK + h)),
            pl.BlockSpec((SEQ, LANES), lambda b, h: (b, DA_K_BLK + h)),
            pl.BlockSpec((SEQ, LANES), lambda b, h: (b, DA_V_BLK + h)),
            pl.BlockSpec((4, DA_QK_DIM), lambda b, h: (0, 0)),
            pl.BlockSpec((1, DA_V_DIM), lambda b, h: (0, 0)),
        ],
        out_specs=pl.BlockSpec((SEQ, LANES), lambda b, h: (b, h)),
        out_shape=jax.ShapeDtypeStruct((TOKENS, DA_HEADS * DA_V_DIM), BF16),
        compiler_params=_params("arbitrary", "arbitrary"),
        name="da_attention",
    )(slopes, proj, proj, proj, lam_vecs, da_norm)


def _conv_silu(x_ref, w_ref):
    x = x_ref[...].astype(F32)
    w = w_ref[...]
    row = lax.broadcasted_iota(jnp.int32, x.shape, 0)
    y = x * w[CONV_WIDTH - 1:CONV_WIDTH]
    for back in range(1, CONV_WIDTH):
        shifted = jnp.where(row >= back, pltpu.roll(x, back, axis=0), 0.0)
        y = y + shifted * w[CONV_WIDTH - 1 - back:CONV_WIDTH - back]
    return y * _sigmoid(y)


def _l2norm_rows(t):
    return t * lax.rsqrt(jnp.sum(t * t, axis=-1, keepdims=True) + L2_EPS)


def _gdn_kernel(q_ref, k_ref, v_ref, z_ref, wq_ref, wk_ref, wv_ref, gtm_ref, ghm_ref, norm_ref,
                o_ref, qs_ref, ks_ref, vs_ref, qp_ref, o0_ref, mt_ref, nt_ref, egl_ref, os_ref):
    h = pl.program_id(1)
    c = GDN_CHUNK
    grp = GDN_GROUP
    qs_ref[...] = _l2norm_rows(_conv_silu(q_ref, wq_ref)) * (GDN_K_DIM ** -0.5)
    ks_ref[...] = _l2norm_rows(_conv_silu(k_ref, wk_ref))
    vs_ref[...] = _conv_silu(v_ref, wv_ref)

    ii = lax.broadcasted_iota(jnp.int32, (grp, grp), 0)
    jj = lax.broadcasted_iota(jnp.int32, (grp, grp), 1)
    sib = jnp.bitwise_xor(ii, jj)
    same = sib < c
    incl = same & (ii >= jj)
    strict = same & (ii > jj)
    last_of_chunk = jj == (ii | (c - 1))
    eye = jnp.where(ii == jj, 1.0, 0.0).astype(F32)

    for g0 in range(0, N_GROUPS, GDN_INTERLEAVE):
        lanes = range(GDN_INTERLEAVE)
        rows = [pl.ds((g0 + x) * grp, grp) for x in lanes]
        q = [qs_ref[r, :] for r in rows]
        k = [ks_ref[r, :] for r in rows]
        v = [vs_ref[r, :] for r in rows]
        gates = [gtm_ref[r, :] for r in rows]
        g_col = [_lane_pick(gt, h) for gt in gates]
        beta = [_lane_pick(gt, GDN_HEADS + h) for gt in gates]
        g_row = [ghm_ref[h, pl.ds(g0 + x, 1), :] for x in lanes]
        decay = [jnp.exp(jnp.where(incl, gc - gr, -jnp.inf)) for gc, gr in zip(g_col, g_row)]
        eg = [jnp.exp(gc) for gc in g_col]
        k_beta = [kx * bx for kx, bx in zip(k, beta)]
        a = [jnp.where(strict, _dot_bf16(kb, kx, NT_DIMS) * dc, 0.0) for kb, kx, dc in zip(k_beta, k, decay)]
        t_inv = [eye - jnp.where(sib == 1, ax, 0.0) for ax in a]
        s = 2
        while s < c:
            coupling = (sib >= s) & (sib < 2 * s)
            at = [_dot_bf16(jnp.where(coupling, ax, 0.0), tx) for ax, tx in zip(a, t_inv)]
            t_inv = [tx - _dot_bf16(tx, atx) for tx, atx in zip(t_inv, at)]
            s *= 2
        wu = [_dot_bf16(tx, jnp.concatenate([kb * ex, vx * bx], axis=1))
              for tx, kb, ex, vx, bx in zip(t_inv, k_beta, eg, v, beta)]
        intra = [_dot_bf16(qx, kx, NT_DIMS) * dc for qx, kx, dc in zip(q, k, decay)]
        iwu = [_dot_bf16(ix, wx) for ix, wx in zip(intra, wu)]
        for x in lanes:
            qp_ref[rows[x], :] = (q[x] * eg[x] - iwu[x][:, :GDN_K_DIM]).astype(BF16)
            o0_ref[rows[x], :] = iwu[x][:, GDN_K_DIM:]
        for x in lanes:
            gl_col = jnp.sum(jnp.where(last_of_chunk, g_row[x], 0.0), axis=1, keepdims=True)
            k_dec = k[x] * jnp.exp(gl_col - g_col[x])
            for ci in range(CHUNKS_PER_GROUP):
                n = (g0 + x) * CHUNKS_PER_GROUP + ci
                zt = _dot_bf16(wu[x][ci * c:(ci + 1) * c, :], k_dec[ci * c:(ci + 1) * c, :], TN_DIMS)
                mt_ref[pl.ds(n * GDN_K_DIM, GDN_K_DIM), :] = zt[:GDN_K_DIM].astype(BF16)
                nt_ref[pl.ds(n * GDN_V_DIM, GDN_V_DIM), :] = zt[GDN_K_DIM:]
                gl = g_row[x][:, (ci + 1) * c - 1:(ci + 1) * c]
                egl_ref[pl.ds(n, 1), :] = jnp.broadcast_to(jnp.exp(gl), (1, LANES))

    def step(n, st):
        r0 = pl.multiple_of(n * c, c)
        m0 = pl.multiple_of(n * GDN_K_DIM, GDN_K_DIM)
        st_b = st.astype(BF16)
        os_ref[pl.ds(r0, c), :] = (
            lax.dot_general(qp_ref[pl.ds(r0, c), :], st_b, NT_DIMS, preferred_element_type=F32)
            + o0_ref[pl.ds(r0, c), :])
        return (st * egl_ref[pl.ds(n, 1), :]
                - jnp.dot(st_b, mt_ref[pl.ds(m0, GDN_K_DIM), :], preferred_element_type=F32)
                + nt_ref[pl.ds(m0, GDN_V_DIM), :])

    lax.fori_loop(0, N_CHUNKS, step, jnp.zeros((GDN_V_DIM, GDN_K_DIM), F32), unroll=2)

    z = z_ref[...].astype(F32)
    o = _rms_rows(os_ref[...], norm_ref[...], NORM_EPS) * (z * _sigmoid(z))
    o_ref[...] = o.astype(o_ref.dtype)


def _gdn(proj, conv_w, gates_tm, g_hm, gdn_norm):
    seq_blk = lambda off: pl.BlockSpec((SEQ, LANES), lambda b, h: (b, off + h))
    conv_blk = lambda off: pl.BlockSpec((CONV_WIDTH, LANES), lambda b, h: (0, off + h))
    return pl.pallas_call(
        _gdn_kernel,
        grid=(BATCH, GDN_HEADS),
        in_specs=[
            seq_blk(GDN_Q_BLK), seq_blk(GDN_K_BLK), seq_blk(GDN_V_BLK), seq_blk(GDN_Z_BLK),
            conv_blk(0), conv_blk(GDN_HEADS), conv_blk(2 * GDN_HEADS),
            pl.BlockSpec((SEQ, LANES), lambda b, h: (b, 0)),
            pl.BlockSpec((GDN_HEADS, N_GROUPS, GDN_GROUP), lambda b, h: (0, b, 0)),
            pl.BlockSpec((1, GDN_V_DIM), lambda b, h: (0, 0)),
        ],
        out_specs=pl.BlockSpec((SEQ, LANES), lambda b, h: (b, h)),
        out_shape=jax.ShapeDtypeStruct((TOKENS, GDN_HEADS * GDN_V_DIM), BF16),
        scratch_shapes=[
            pltpu.VMEM((SEQ, LANES), F32),
            pltpu.VMEM((SEQ, LANES), F32),
            pltpu.VMEM((SEQ, LANES), F32),
            pltpu.VMEM((SEQ, LANES), BF16),
            pltpu.VMEM((SEQ, LANES), F32),
            pltpu.VMEM((N_CHUNKS * GDN_K_DIM, LANES), BF16),
            pltpu.VMEM((N_CHUNKS * GDN_V_DIM, LANES), F32),
            pltpu.VMEM((N_CHUNKS, LANES), F32),
            pltpu.VMEM((SEQ, LANES), F32),
        ],
        compiler_params=_params("arbitrary", "arbitrary"),
        name="gdn",
    )(proj, proj, proj, proj, conv_w, conv_w, conv_w, gates_tm, g_hm, gdn_norm)


def _out_proj_kernel(xa_ref, xb_ref, wa_ref, wb_ref, h_ref, o_ref):
    o_ref[...] = (h_ref[...]
                  + jnp.dot(xa_ref[...], wa_ref[...].astype(BF16), preferred_element_type=F32)
                  + jnp.dot(xb_ref[...], wb_ref[...].astype(BF16), preferred_element_type=F32))


def _out_proj(xa, xb_spec_col, xb, w_stack, idx, h):
    half = w_stack.shape[1] // 2
    return pl.pallas_call(
        _out_proj_kernel,
        grid=(TOKENS // OUT_TM, D_MODEL // OUT_TN),
        in_specs=[
            pl.BlockSpec((OUT_TM, half), lambda i, j: (i, 0)),
            pl.BlockSpec((OUT_TM, half), lambda i, j: (i, xb_spec_col)),
            pl.BlockSpec((None, half, OUT_TN), lambda i, j: (idx, 0, j)),
            pl.BlockSpec((None, half, OUT_TN), lambda i, j: (idx, 1, j)),
            pl.BlockSpec((OUT_TM, OUT_TN), lambda i, j: (i, j)),
        ],
        out_specs=pl.BlockSpec((OUT_TM, OUT_TN), lambda i, j: (i, j)),
        out_shape=jax.ShapeDtypeStruct((TOKENS, D_MODEL), F32),
        compiler_params=_params("arbitrary", "arbitrary"),
        name="out_proj",
    )(xa, xb, w_stack, w_stack, h)


def _mlp_kernel(h_ref, g_ref, wu_ref, wd_ref, gf_ref, o_ref, xn_ref, *, final_norm):
    f = pl.program_id(1)

    @pl.when(f == 0)
    def _():
        x = h_ref[...]
        xn_ref[...] = _rms_rows(x, g_ref[...], NORM_EPS).astype(BF16)
        o_ref[...] = x

    a = jnp.maximum(jnp.dot(xn_ref[...], wu_ref[...].astype(BF16), preferred_element_type=F32), 0.0)
    o_ref[...] += jnp.dot((a * a).astype(BF16), wd_ref[...].astype(BF16), preferred_element_type=F32)

    if final_norm:
        @pl.when(f == pl.num_programs(1) - 1)
        def _():
            o_ref[...] = _rms_rows(o_ref[...], gf_ref[...], NORM_EPS)


def _mlp(h, g, w_up, w_down, layer, g_final, final_norm):
    return pl.pallas_call(
        functools.partial(_mlp_kernel, final_norm=final_norm),
        grid=(TOKENS // MLP_TM, D_FF // MLP_TF),
        in_specs=[
            pl.BlockSpec((MLP_TM, D_MODEL), lambda i, f: (i, 0), pipeline_mode=pl.Buffered(1)),
            pl.BlockSpec((1, D_MODEL), lambda i, f: (0, 0)),
            pl.BlockSpec((None, D_MODEL, MLP_TF), lambda i, f: (layer, 0, f)),
            pl.BlockSpec((None, MLP_TF, D_MODEL), lambda i, f: (layer, f, 0)),
            pl.BlockSpec((1, D_MODEL), lambda i, f: (0, 0)),
        ],
        out_specs=pl.BlockSpec((MLP_TM, D_MODEL), lambda i, f: (i, 0)),
        out_shape=jax.ShapeDtypeStruct((TOKENS, D_MODEL), F32),
        scratch_shapes=[pltpu.VMEM((MLP_TM, D_MODEL), BF16)],
        compiler_params=_params("arbitrary", "arbitrary"),
        name="mlp",
    )(h, g, w_up, w_down, g_final)


def _lanes(vec, offset=0):
    return jnp.zeros((1, LANES), F32).at[0, offset:offset + vec.shape[0]].set(vec.astype(F32))


def _side_weight(w):
    return jnp.pad(w, ((0, 0), (0, LANES - w.shape[1]))).astype(BF16)


def _even_mixer(h, g_mix, w_in, idx, conv_w, lam_vecs, da_norm, a_log, dt_bias, gdn_norm, w_out, layer):
    proj, side = _in_proj(h, g_mix, w_in, idx, EVEN_MAIN, _side_weight(w_in[idx, :, EVEN_MAIN:]))
    lam_init = 0.8 - 0.6 * math.exp(-0.3 * layer)
    slopes = 2.0 ** (-8.0 * jnp.arange(1, DA_HEADS + 1, dtype=F32) / DA_HEADS)
    o_a = _da_attention(proj, slopes, lam_vecs, da_norm.reshape(1, DA_V_DIM), lam_init)

    gates_tm = _gdn_gates(side, _lanes(a_log), _lanes(dt_bias))
    g_hm = gates_tm[:, :GDN_HEADS].T.reshape(GDN_HEADS, BATCH * N_GROUPS, GDN_GROUP)
    o_b = _gdn(proj, conv_w, gates_tm, g_hm, gdn_norm.reshape(1, GDN_V_DIM))
    return _out_proj(o_a, 0, o_b, w_out, idx, h)


def _odd_mixer(h, g_mix, w_in, idx, b_f, w_out):
    qkv, side = _in_proj(h, g_mix, w_in, idx, ODD_MAIN, _side_weight(w_in[idx, :, ODD_MAIN:]))
    c_tm = _fox_gates(side, _lanes(b_f))
    c_hm = c_tm.reshape(BATCH, SEQ, LANES)[:, :, :FOX_HEADS].transpose(0, 2, 1)
    o = _fox_attention(qkv, c_tm, c_hm)
    return _out_proj(o, 1, o, w_out, idx, h)


def kernel(x, norm_mix, norm_mlp, norm_final, w_in_even, conv_w, lam_q1, lam_k1, lam_q2, lam_k2,
           da_norm, gdn_a_log, gdn_dt_bias, gdn_norm, w_out_even, w_in_odd, fox_b_f, w_out_odd,
           w_up, w_down):
    h = x.reshape(TOKENS, D_MODEL)
    g_final = norm_final.reshape(1, D_MODEL)
    for layer in range(DEPTH):
        i = layer // 2
        g_mix = norm_mix[layer].reshape(1, D_MODEL)
        if layer % 2 == 0:
            lam_vecs = jnp.stack([lam_q1[i], lam_k1[i], lam_q2[i], lam_k2[i]]).astype(F32)
            h = _even_mixer(h, g_mix, w_in_even, i, conv_w[i], lam_vecs, da_norm[i], gdn_a_log[i],
                            gdn_dt_bias[i], gdn_norm[i], w_out_even, layer)
        else:
            h = _odd_mixer(h, g_mix, w_in_odd, i, fox_b_f[i], w_out_odd)
        h = _mlp(h, norm_mlp[layer].reshape(1, D_MODEL), w_up, w_down, layer, g_final, layer == DEPTH - 1)
    return h.reshape(BATCH, SEQ, D_MODEL)
```

```python
import functools
import math

import jax
import jax.numpy as jnp
from jax import lax
from jax.experimental import pallas as pl
from jax.experimental.pallas import tpu as pltpu

F32 = jnp.float32
BF16 = jnp.bfloat16

D_MODEL = 2048
BATCH = 4
SEQ = 2048
DEPTH = 4
TOKENS = BATCH * SEQ

DA_HEADS = 8
DA_QK_DIM = 64
DA_V_DIM = 128
GDN_HEADS = 8
GDN_K_DIM = 128
GDN_V_DIM = 128
GDN_CHUNK = 64
CONV_WIDTH = 4
FOX_HEADS = 16
FOX_HEAD_DIM = 128
D_FF = 4 * D_MODEL
NORM_EPS = 1e-6
L2_EPS = 1e-6

LANES = 128
VMEM_LIMIT = 56 * 1024 * 1024

EVEN_MAIN = 3 * 1024 + 3 * 1024 + 1024
DA_Q_BLK, DA_K_BLK, DA_V_BLK = 0, 8, 16
GDN_Q_BLK, GDN_K_BLK, GDN_V_BLK, GDN_Z_BLK = 24, 32, 40, 48
ODD_MAIN = 3 * FOX_HEADS * FOX_HEAD_DIM
FOX_K_BLK, FOX_V_BLK = 16, 32

PROJ_TM, PROJ_TN = 1024, 1024
MLP_TM, MLP_TF = 1024, 512
OUT_TM, OUT_TN = 1024, 512
GATE_TM = 256
FOX_TQ, FOX_TK_WIDE = 256, 256
DA_TQ, DA_TK_WIDE = 256, 256
N_CHUNKS = SEQ // GDN_CHUNK
GDN_GROUP = 256
CHUNKS_PER_GROUP = GDN_GROUP // GDN_CHUNK
N_GROUPS = SEQ // GDN_GROUP
GDN_INTERLEAVE = 8

LOG2E = math.log2(math.e)
DA_QK_SCALE = DA_QK_DIM ** -0.5 * LOG2E
FOX_QK_SCALE = FOX_HEAD_DIM ** -0.5 * LOG2E

NT_DIMS = (((1,), (1,)), ((), ()))
TN_DIMS = (((0,), (0,)), ((), ()))


def _params(*sem):
    return pltpu.CompilerParams(dimension_semantics=sem, vmem_limit_bytes=VMEM_LIMIT)


def _rms_rows(x, g, eps):
    ms = jnp.mean(x * x, axis=-1, keepdims=True)
    return x * lax.rsqrt(ms + eps) * g


def _dot_bf16(a, b, dims=None):
    a = a.astype(BF16)
    b = b.astype(BF16)
    if dims is None:
        return jnp.dot(a, b, preferred_element_type=F32)
    return lax.dot_general(a, b, dims, preferred_element_type=F32)


def _dot_f32(a, b):
    return jnp.dot(a, b, precision=lax.Precision.HIGHEST, preferred_element_type=F32)


def _softplus(x):
    return jnp.maximum(x, 0.0) + jnp.log1p(jnp.exp(-jnp.abs(x)))


def _sigmoid(x):
    return 1.0 / (1.0 + jnp.exp(-x))


def _proj_kernel(x_ref, g_ref, w_ref, cs_ref, ws_ref, o_ref, os_ref, xn_ref):
    @pl.when(pl.program_id(1) == 0)
    def _():
        xn = _rms_rows(x_ref[...], g_ref[...], NORM_EPS).astype(BF16)
        xn_ref[...] = xn
        os_ref[...] = jnp.dot(xn, ws_ref[...].astype(BF16), preferred_element_type=F32)

    w = (w_ref[...] * cs_ref[...]).astype(BF16)
    o_ref[...] = jnp.dot(xn_ref[...], w, preferred_element_type=F32).astype(o_ref.dtype)


def _in_proj(h, g, w_stack, idx, n, col_scale, w_side):
    grid = (TOKENS // PROJ_TM, n // PROJ_TN)
    return pl.pallas_call(
        _proj_kernel,
        grid=grid,
        in_specs=[
            pl.BlockSpec((PROJ_TM, D_MODEL), lambda i, j: (i, 0)),
            pl.BlockSpec((1, D_MODEL), lambda i, j: (0, 0)),
            pl.BlockSpec((None, D_MODEL, PROJ_TN), lambda i, j: (idx, 0, j)),
            pl.BlockSpec((1, PROJ_TN), lambda i, j: (0, j)),
            pl.BlockSpec((D_MODEL, LANES), lambda i, j: (0, 0)),
        ],
        out_specs=[
            pl.BlockSpec((PROJ_TM, PROJ_TN), lambda i, j: (i, j)),
            pl.BlockSpec((PROJ_TM, LANES), lambda i, j: (i, 0)),
        ],
        out_shape=[
            jax.ShapeDtypeStruct((TOKENS, n), BF16),
            jax.ShapeDtypeStruct((TOKENS, LANES), F32),
        ],
        scratch_shapes=[pltpu.VMEM((PROJ_TM, D_MODEL), BF16)],
        compiler_params=_params("arbitrary", "arbitrary"),
        name="in_proj",
    )(h, g, w_stack, col_scale, w_side)


def _same_chunk(r, c, chunk):
    return jnp.bitwise_xor(r, c) < chunk


def _tri_ones(n, chunk):
    r = lax.broadcasted_iota(jnp.int32, (n, n), 0)
    c = lax.broadcasted_iota(jnp.int32, (n, n), 1)
    return jnp.where((c <= r) & _same_chunk(r, c, chunk), 1.0, 0.0).astype(F32)


def _fox_gate_kernel(s_ref, bf_ref, c_ref, carry_ref):
    @pl.when(pl.program_id(1) == 0)
    def _():
        carry_ref[...] = jnp.zeros_like(carry_ref)

    x = s_ref[...] + bf_ref[...]
    log_f = jnp.minimum(x, 0.0) - jnp.log1p(jnp.exp(-jnp.abs(x)))
    c = _dot_f32(_tri_ones(GATE_TM, GATE_TM), log_f) + carry_ref[...]
    c_ref[...] = c
    carry_ref[...] = c[GATE_TM - 1:GATE_TM, :]


def _fox_gates(side, b_f_lanes):
    return pl.pallas_call(
        _fox_gate_kernel,
        grid=(BATCH, SEQ // GATE_TM),
        in_specs=[
            pl.BlockSpec((GATE_TM, LANES), lambda b, t: (b * (SEQ // GATE_TM) + t, 0)),
            pl.BlockSpec((1, LANES), lambda b, t: (0, 0)),
        ],
        out_specs=pl.BlockSpec((GATE_TM, LANES), lambda b, t: (b * (SEQ // GATE_TM) + t, 0)),
        out_shape=jax.ShapeDtypeStruct((TOKENS, LANES), F32),
        scratch_shapes=[pltpu.VMEM((1, LANES), F32)],
        compiler_params=_params("arbitrary", "arbitrary"),
        name="fox_gates",
    )(side, b_f_lanes)


def _gdn_gate_kernel(s_ref, alog_ref, dtb_ref, o_ref):
    x = s_ref[...]
    log_decay = -jnp.exp(alog_ref[...]) * _softplus(x + dtb_ref[...])
    g = _dot_f32(_tri_ones(GATE_TM, GDN_CHUNK), log_decay)
    lane = lax.broadcasted_iota(jnp.int32, x.shape, 1)
    o_ref[...] = jnp.where(lane < GDN_HEADS, g, _sigmoid(x))


def _gdn_gates(side, a_log_lanes, dt_bias_lanes):
    return pl.pallas_call(
        _gdn_gate_kernel,
        grid=(TOKENS // GATE_TM,),
        in_specs=[
            pl.BlockSpec((GATE_TM, LANES), lambda t: (t, 0)),
            pl.BlockSpec((1, LANES), lambda t: (0, 0)),
            pl.BlockSpec((1, LANES), lambda t: (0, 0)),
        ],
        out_specs=pl.BlockSpec((GATE_TM, LANES), lambda t: (t, 0)),
        out_shape=jax.ShapeDtypeStruct((TOKENS, LANES), F32),
        compiler_params=_params("arbitrary"),
        name="gdn_gates",
    )(side, a_log_lanes, dt_bias_lanes)


def _lane_pick(tile, lane_idx):
    lane = lax.broadcasted_iota(jnp.int32, tile.shape, 1)
    return jnp.sum(jnp.where(lane == lane_idx, tile, 0.0), axis=1, keepdims=True)


def _flash_static(q_of, k_ref, v_ref, cq_of, ck_of, emit, tq, tk_wide):
    for i in range(SEQ // tq):
        q = q_of(i)
        rows = q.shape[0]
        cq = cq_of(i)
        blocks, pos = [], 0
        while pos < i * tq:
            width = tk_wide if pos + tk_wide <= i * tq else tq
            blocks.append((pos, width, False))
            pos += width
        blocks.append((i * tq, tq, True))
        m = l = acc = None
        for j, (k0, width, diagonal) in enumerate(blocks):
            k = k_ref[k0:k0 + width, :]
            v = v_ref[k0:k0 + width, :]
            a = lax.dot_general(q, k, NT_DIMS, preferred_element_type=F32) - ck_of(k0, width)
            if diagonal:
                qpos = lax.broadcasted_iota(jnp.int32, (rows, width), 0) & (tq - 1)
                kpos = lax.broadcasted_iota(jnp.int32, (rows, width), 1)
                a = jnp.where(qpos >= kpos, a, -jnp.inf)
            m_blk = jnp.max(a, axis=1, keepdims=True) + cq
            if j == 0:
                m = m_blk
                p = jnp.exp2(a - (m - cq))
                l = jnp.sum(p, axis=1, keepdims=True)
                acc = jnp.dot(p.astype(BF16), v, preferred_element_type=F32)
            else:
                m_new = jnp.maximum(m, m_blk)
                alpha = jnp.exp2(m - m_new)
                p = jnp.exp2(a - (m_new - cq))
                l = alpha * l + jnp.sum(p, axis=1, keepdims=True)
                acc = alpha * acc + jnp.dot(p.astype(BF16), v, preferred_element_type=F32)
                m = m_new
        emit(i, acc / l)


def _fox_attn_kernel(q_ref, k_ref, v_ref, ctm_ref, chm_ref, o_ref):
    h = pl.program_id(1)

    def q_of(i):
        return q_ref[i * FOX_TQ:(i + 1) * FOX_TQ, :]

    def cq_of(i):
        return _lane_pick(ctm_ref[i * FOX_TQ:(i + 1) * FOX_TQ, :], h) * LOG2E

    def ck_of(k0, width):
        return chm_ref[pl.ds(h, 1), k0:k0 + width] * LOG2E

    def emit(i, o):
        o_ref[i * FOX_TQ:(i + 1) * FOX_TQ, :] = o.astype(o_ref.dtype)

    _flash_static(q_of, k_ref, v_ref, cq_of, ck_of, emit, FOX_TQ, FOX_TK_WIDE)


def _fox_attention(qkv, c_tm, c_hm):
    return pl.pallas_call(
        _fox_attn_kernel,
        grid=(BATCH, FOX_HEADS),
        in_specs=[
            pl.BlockSpec((SEQ, LANES), lambda b, h: (b, h)),
            pl.BlockSpec((SEQ, LANES), lambda b, h: (b, FOX_K_BLK + h)),
            pl.BlockSpec((SEQ, LANES), lambda b, h: (b, FOX_V_BLK + h)),
            pl.BlockSpec((SEQ, LANES), lambda b, h: (b, 0)),
            pl.BlockSpec((None, FOX_HEADS, SEQ), lambda b, h: (b, 0, 0)),
        ],
        out_specs=pl.BlockSpec((SEQ, LANES), lambda b, h: (b, h)),
        out_shape=jax.ShapeDtypeStruct((TOKENS, FOX_HEADS * FOX_HEAD_DIM), BF16),
        compiler_params=_params("arbitrary", "arbitrary"),
        name="fox_attention",
    )(qkv, qkv, qkv, c_tm, c_hm)


def _da_attn_kernel(slope_ref, q_ref, k_ref, v_ref, lam_ref, norm_ref, o_ref, *, lam_init):
    h = pl.program_id(1)
    c_per_pos = -slope_ref[h] * LOG2E

    lv = lam_ref[...]
    lam = (jnp.exp(jnp.sum(lv[0:1] * lv[1:2], axis=1, keepdims=True))
           - jnp.exp(jnp.sum(lv[2:3] * lv[3:4], axis=1, keepdims=True)) + lam_init)

    def q_of(i):
        q = q_ref[i * DA_TQ:(i + 1) * DA_TQ, :]
        lane = lax.broadcasted_iota(jnp.int32, q.shape, 1)
        zero = jnp.zeros_like(q)
        return jnp.concatenate([jnp.where(lane < DA_QK_DIM, q, zero),
                                jnp.where(lane >= DA_QK_DIM, q, zero)], axis=0)

    def cq_of(i):
        pos = i * DA_TQ + (lax.broadcasted_iota(jnp.int32, (2 * DA_TQ, 1), 0) & (DA_TQ - 1))
        return c_per_pos * pos.astype(F32)

    def ck_of(k0, width):
        pos = k0 + lax.broadcasted_iota(jnp.int32, (1, width), 1)
        return c_per_pos * pos.astype(F32)

    def emit(i, o2):
        o = o2[:DA_TQ] - lam * o2[DA_TQ:]
        o = _rms_rows(o, norm_ref[...], NORM_EPS) * (1.0 - lam_init)
        o_ref[i * DA_TQ:(i + 1) * DA_TQ, :] = o.astype(o_ref.dtype)

    _flash_static(q_of, k_ref, v_ref, cq_of, ck_of, emit, DA_TQ, DA_TK_WIDE)


def _da_attention(proj, slopes, lam_vecs, da_norm, lam_init):
    return pl.pallas_call(
        functools.partial(_da_attn_kernel, lam_init=lam_init),
        grid=(BATCH, DA_HEADS),
        in_specs=[
            pl.BlockSpec(memory_space=pltpu.SMEM),
            pl.BlockSpec((SEQ, LANES), lambda b, h: (b, DA_Q_BLK + h)),
            pl.BlockSpec((SEQ, LANES), lambda b, h: (b, DA_K_BLK + h)),
            pl.BlockSpec((SEQ, LANES), lambda b, h: (b, DA_V_BLK + h)),
            pl.BlockSpec((4, DA_QK_DIM), lambda b, h: (0, 0)),
            pl.BlockSpec((1, DA_V_DIM), lambda b, h: (0, 0)),
        ],
        out_specs=pl.BlockSpec((SEQ, LANES), lambda b, h: (b, h)),
        out_shape=jax.ShapeDtypeStruct((TOKENS, DA_HEADS * DA_V_DIM), BF16),
        compiler_params=_params("arbitrary", "arbitrary"),
        name="da_attention",
    )(slopes, proj, proj, proj, lam_vecs, da_norm)


def _conv_silu(x_ref, w_ref):
    x = x_ref[...].astype(F32)
    w = w_ref[...]
    row = lax.broadcasted_iota(jnp.int32, x.shape, 0)
    y = x * w[CONV_WIDTH - 1:CONV_WIDTH]
    for back in range(1, CONV_WIDTH):
        shifted = jnp.where(row >= back, pltpu.roll(x, back, axis=0), 0.0)
        y = y + shifted * w[CONV_WIDTH - 1 - back:CONV_WIDTH - back]
    return y * _sigmoid(y)


def _l2norm_rows(t):
    return t * lax.rsqrt(jnp.sum(t * t, axis=-1, keepdims=True) + L2_EPS)


def _gdn_kernel(q_ref, k_ref, v_ref, z_ref, wq_ref, wk_ref, wv_ref, gtm_ref, ghm_ref, norm_ref,
                o_ref, qs_ref, ks_ref, vs_ref, qp_ref, o0_ref, mt_ref, nt_ref, egl_ref, os_ref):
    h = pl.program_id(1)
    c = GDN_CHUNK
    grp = GDN_GROUP
    qs_ref[...] = _l2norm_rows(_conv_silu(q_ref, wq_ref)) * (GDN_K_DIM ** -0.5)
    ks_ref[...] = _l2norm_rows(_conv_silu(k_ref, wk_ref))
    vs_ref[...] = _conv_silu(v_ref, wv_ref)

    ii = lax.broadcasted_iota(jnp.int32, (grp, grp), 0)
    jj = lax.broadcasted_iota(jnp.int32, (grp, grp), 1)
    sib = jnp.bitwise_xor(ii, jj)
    same = sib < c
    incl = same & (ii >= jj)
    strict = same & (ii > jj)
    last_of_chunk = jj == (ii | (c - 1))
    eye = jnp.where(ii == jj, 1.0, 0.0).astype(F32)

    for g0 in range(0, N_GROUPS, GDN_INTERLEAVE):
        lanes = range(GDN_INTERLEAVE)
        rows = [pl.ds((g0 + x) * grp, grp) for x in lanes]
        q = [qs_ref[r, :] for r in rows]
        k = [ks_ref[r, :] for r in rows]
        v = [vs_ref[r, :] for r in rows]
        gates = [gtm_ref[r, :] for r in rows]
        g_col = [_lane_pick(gt, h) for gt in gates]
        beta = [_lane_pick(gt, GDN_HEADS + h) for gt in gates]
        g_row = [ghm_ref[h, pl.ds(g0 + x, 1), :] for x in lanes]
        decay = [jnp.exp(jnp.where(incl, gc - gr, -jnp.inf)) for gc, gr in zip(g_col, g_row)]
        eg = [jnp.exp(gc) for gc in g_col]
        k_beta = [kx * bx for kx, bx in zip(k, beta)]
        a = [jnp.where(strict, _dot_bf16(kb, kx, NT_DIMS) * dc, 0.0) for kb, kx, dc in zip(k_beta, k, decay)]
        t_inv = [eye - jnp.where(sib == 1, ax, 0.0) for ax in a]
        s = 2
        while s < c:
            coupling = (sib >= s) & (sib < 2 * s)
            at = [_dot_bf16(jnp.where(coupling, ax, 0.0), tx) for ax, tx in zip(a, t_inv)]
            t_inv = [tx - _dot_bf16(tx, atx) for tx, atx in zip(t_inv, at)]
            s *= 2
        wu = [_dot_bf16(tx, jnp.concatenate([kb * ex, vx * bx], axis=1))
              for tx, kb, ex, vx, bx in zip(t_inv, k_beta, eg, v, beta)]
        intra = [_dot_bf16(qx, kx, NT_DIMS) * dc for qx, kx, dc in zip(q, k, decay)]
        iwu = [_dot_bf16(ix, wx) for ix, wx in zip(intra, wu)]
        for x in lanes:
            qp_ref[rows[x], :] = (q[x] * eg[x] - iwu[x][:, :GDN_K_DIM]).astype(BF16)
            o0_ref[rows[x], :] = iwu[x][:, GDN_K_DIM:]
        for x in lanes:
            gl_col = jnp.sum(jnp.where(last_of_chunk, g_row[x], 0.0), axis=1, keepdims=True)
            k_dec = k[x] * jnp.exp(gl_col - g_col[x])
            for ci in range(CHUNKS_PER_GROUP):
                n = (g0 + x) * CHUNKS_PER_GROUP + ci
                zt = _dot_bf16(wu[x][ci * c:(ci + 1) * c, :], k_dec[ci * c:(ci + 1) * c, :], TN_DIMS)
                mt_ref[pl.ds(n * GDN_K_DIM, GDN_K_DIM), :] = zt[:GDN_K_DIM].astype(BF16)
                nt_ref[pl.ds(n * GDN_V_DIM, GDN_V_DIM), :] = zt[GDN_K_DIM:]
                gl = g_row[x][:, (ci + 1) * c - 1:(ci + 1) * c]
                egl_ref[pl.ds(n, 1), :] = jnp.broadcast_to(jnp.exp(gl), (1, LANES))

    def step(n, st):
        r0 = pl.multiple_of(n * c, c)
        m0 = pl.multiple_of(n * GDN_K_DIM, GDN_K_DIM)
        st_b = st.astype(BF16)
        os_ref[pl.ds(r0, c), :] = (
            lax.dot_general(qp_ref[pl.ds(r0, c), :], st_b, NT_DIMS, preferred_element_type=F32)
            + o0_ref[pl.ds(r0, c), :])
        return (st * egl_ref[pl.ds(n, 1), :]
                - jnp.dot(st_b, mt_ref[pl.ds(m0, GDN_K_DIM), :], preferred_element_type=F32)
                + nt_ref[pl.ds(m0, GDN_V_DIM), :])

    lax.fori_loop(0, N_CHUNKS, step, jnp.zeros((GDN_V_DIM, GDN_K_DIM), F32), unroll=2)

    z = z_ref[...].astype(F32)
    o = _rms_rows(os_ref[...], norm_ref[...], NORM_EPS) * (z * _sigmoid(z))
    o_ref[...] = o.astype(o_ref.dtype)


def _gdn(proj, conv_w, gates_tm, g_hm, gdn_norm):
    seq_blk = lambda off: pl.BlockSpec((SEQ, LANES), lambda b, h: (b, off + h))
    conv_blk = lambda off: pl.BlockSpec((CONV_WIDTH, LANES), lambda b, h: (0, off + h))
    return pl.pallas_call(
        _gdn_kernel,
        grid=(BATCH, GDN_HEADS),
        in_specs=[
            seq_blk(GDN_Q_BLK), seq_blk(GDN_K_BLK), seq_blk(GDN_V_BLK), seq_blk(GDN_Z_BLK),
            conv_blk(0), conv_blk(GDN_HEADS), conv_blk(2 * GDN_HEADS),
            pl.BlockSpec((SEQ, LANES), lambda b, h: (b, 0)),
            pl.BlockSpec((GDN_HEADS, N_GROUPS, GDN_GROUP), lambda b, h: (0, b, 0)),
            pl.BlockSpec((1, GDN_V_DIM), lambda b, h: (0, 0)),
        ],
        out_specs=pl.BlockSpec((SEQ, LANES), lambda b, h: (b, h)),
        out_shape=jax.ShapeDtypeStruct((TOKENS, GDN_HEADS * GDN_V_DIM), BF16),
        scratch_shapes=[
            pltpu.VMEM((SEQ, LANES), F32),
            pltpu.VMEM((SEQ, LANES), F32),
            pltpu.VMEM((SEQ, LANES), F32),
            pltpu.VMEM((SEQ, LANES), BF16),
            pltpu.VMEM((SEQ, LANES), F32),
            pltpu.VMEM((N_CHUNKS * GDN_K_DIM, LANES), BF16),
            pltpu.VMEM((N_CHUNKS * GDN_V_DIM, LANES), F32),
            pltpu.VMEM((N_CHUNKS, LANES), F32),
            pltpu.VMEM((SEQ, LANES), F32),
        ],
        compiler_params=_params("arbitrary", "arbitrary"),
        name="gdn",
    )(proj, proj, proj, proj, conv_w, conv_w, conv_w, gates_tm, g_hm, gdn_norm)


def _out_proj_kernel(xa_ref, xb_ref, wa_ref, wb_ref, h_ref, o_ref):
    o_ref[...] = (h_ref[...]
                  + jnp.dot(xa_ref[...], wa_ref[...].astype(BF16), preferred_element_type=F32)
                  + jnp.dot(xb_ref[...], wb_ref[...].astype(BF16), preferred_element_type=F32))


def _out_proj(xa, xb_spec_col, xb, w_stack, idx, h):
    half = w_stack.shape[1] // 2
    return pl.pallas_call(
        _out_proj_kernel,
        grid=(TOKENS // OUT_TM, D_MODEL // OUT_TN),
        in_specs=[
            pl.BlockSpec((OUT_TM, half), lambda i, j: (i, 0)),
            pl.BlockSpec((OUT_TM, half), lambda i, j: (i, xb_spec_col)),
            pl.BlockSpec((None, half, OUT_TN), lambda i, j: (idx, 0, j)),
            pl.BlockSpec((None, half, OUT_TN), lambda i, j: (idx, 1, j)),
            pl.BlockSpec((OUT_TM, OUT_TN), lambda i, j: (i, j)),
        ],
        out_specs=pl.BlockSpec((OUT_TM, OUT_TN), lambda i, j: (i, j)),
        out_shape=jax.ShapeDtypeStruct((TOKENS, D_MODEL), F32),
        compiler_params=_params("arbitrary", "arbitrary"),
        name="out_proj",
    )(xa, xb, w_stack, w_stack, h)


def _mlp_kernel(h_ref, g_ref, wu_ref, wd_ref, gf_ref, o_ref, xn_ref, *, final_norm):
    f = pl.program_id(1)

    @pl.when(f == 0)
    def _():
        x = h_ref[...]
        xn_ref[...] = _rms_rows(x, g_ref[...], NORM_EPS).astype(BF16)
        o_ref[...] = x

    a = jnp.maximum(jnp.dot(xn_ref[...], wu_ref[...].astype(BF16), preferred_element_type=F32), 0.0)
    o_ref[...] += jnp.dot((a * a).astype(BF16), wd_ref[...].astype(BF16), preferred_element_type=F32)

    if final_norm:
        @pl.when(f == pl.num_programs(1) - 1)
        def _():
            o_ref[...] = _rms_rows(o_ref[...], gf_ref[...], NORM_EPS)


def _mlp(h, g, w_up, w_down, layer, g_final, final_norm):
    return pl.pallas_call(
        functools.partial(_mlp_kernel, final_norm=final_norm),
        grid=(TOKENS // MLP_TM, D_FF // MLP_TF),
        in_specs=[
            pl.BlockSpec((MLP_TM, D_MODEL), lambda i, f: (i, 0)),
            pl.BlockSpec((1, D_MODEL), lambda i, f: (0, 0)),
            pl.BlockSpec((None, D_MODEL, MLP_TF), lambda i, f: (layer, 0, f)),
            pl.BlockSpec((None, MLP_TF, D_MODEL), lambda i, f: (layer, f, 0)),
            pl.BlockSpec((1, D_MODEL), lambda i, f: (0, 0)),
        ],
        out_specs=pl.BlockSpec((MLP_TM, D_MODEL), lambda i, f: (i, 0)),
        out_shape=jax.ShapeDtypeStruct((TOKENS, D_MODEL), F32),
        scratch_shapes=[pltpu.VMEM((MLP_TM, D_MODEL), BF16)],
        compiler_params=_params("arbitrary", "arbitrary"),
        name="mlp",
    )(h, g, w_up, w_down, g_final)


def _lanes(vec, offset=0):
    return jnp.zeros((1, LANES), F32).at[0, offset:offset + vec.shape[0]].set(vec.astype(F32))


def _side_weight(w):
    return jnp.pad(w, ((0, 0), (0, LANES - w.shape[1])))


def _q_col_scale(n, q_cols, scale):
    return jnp.where(jnp.arange(n) < q_cols, scale, 1.0).astype(F32).reshape(1, n)


def _even_mixer(h, g_mix, w_in, idx, conv_w, lam_vecs, da_norm, a_log, dt_bias, gdn_norm, w_out, layer):
    proj, side = _in_proj(h, g_mix, w_in, idx, EVEN_MAIN,
                          _q_col_scale(EVEN_MAIN, DA_HEADS * 2 * DA_QK_DIM, DA_QK_SCALE),
                          _side_weight(w_in[idx, :, EVEN_MAIN:]))
    lam_init = 0.8 - 0.6 * math.exp(-0.3 * layer)
    slopes = 2.0 ** (-8.0 * jnp.arange(1, DA_HEADS + 1, dtype=F32) / DA_HEADS)
    o_a = _da_attention(proj, slopes, lam_vecs, da_norm.reshape(1, DA_V_DIM), lam_init)

    gates_tm = _gdn_gates(side, _lanes(a_log), _lanes(dt_bias))
    g_hm = gates_tm[:, :GDN_HEADS].T.reshape(GDN_HEADS, BATCH * N_GROUPS, GDN_GROUP)
    o_b = _gdn(proj, conv_w, gates_tm, g_hm, gdn_norm.reshape(1, GDN_V_DIM))
    return _out_proj(o_a, 0, o_b, w_out, idx, h)


def _odd_mixer(h, g_mix, w_in, idx, b_f, w_out):
    qkv, side = _in_proj(h, g_mix, w_in, idx, ODD_MAIN,
                         _q_col_scale(ODD_MAIN, FOX_HEADS * FOX_HEAD_DIM, FOX_QK_SCALE),
                         _side_weight(w_in[idx, :, ODD_MAIN:]))
    c_tm = _fox_gates(side, _lanes(b_f))
    c_hm = c_tm.reshape(BATCH, SEQ, LANES)[:, :, :FOX_HEADS].transpose(0, 2, 1)
    o = _fox_attention(qkv, c_tm, c_hm)
    return _out_proj(o, 1, o, w_out, idx, h)


def kernel(x, norm_mix, norm_mlp, norm_final, w_in_even, conv_w, lam_q1, lam_k1, lam_q2, lam_k2,
           da_norm, gdn_a_log, gdn_dt_bias, gdn_norm, w_out_even, w_in_odd, fox_b_f, w_out_odd,
           w_up, w_down):
    h = x.reshape(TOKENS, D_MODEL)
    g_final = norm_final.reshape(1, D_MODEL)
    for layer in range(DEPTH):
        i = layer // 2
        g_mix = norm_mix[layer].reshape(1, D_MODEL)
        if layer % 2 == 0:
            lam_vecs = jnp.stack([lam_q1[i], lam_k1[i], lam_q2[i], lam_k2[i]]).astype(F32)
            h = _even_mixer(h, g_mix, w_in_even, i, conv_w[i], lam_vecs, da_norm[i], gdn_a_log[i],
                            gdn_dt_bias[i], gdn_norm[i], w_out_even, layer)
        else:
            h = _odd_mixer(h, g_mix, w_in_odd, i, fox_b_f[i], w_out_odd)
        h = _mlp(h, norm_mlp[layer].reshape(1, D_MODEL), w_up, w_down, layer, g_final, layer == DEPTH - 1)
    return h.reshape(BATCH, SEQ, D_MODEL)
```

```python
import functools
import math

import jax
import jax.numpy as jnp
from jax import lax
from jax.experimental import pallas as pl
from jax.experimental.pallas import tpu as pltpu

F32 = jnp.float32
BF16 = jnp.bfloat16

D_MODEL = 2048
BATCH = 4
SEQ = 2048
DEPTH = 4
TOKENS = BATCH * SEQ

DA_HEADS = 8
DA_QK_DIM = 64
DA_V_DIM = 128
GDN_HEADS = 8
GDN_K_DIM = 128
GDN_V_DIM = 128
GDN_CHUNK = 64
CONV_WIDTH = 4
FOX_HEADS = 16
FOX_HEAD_DIM = 128
D_FF = 4 * D_MODEL
NORM_EPS = 1e-6
L2_EPS = 1e-6

LANES = 128
VMEM_LIMIT = 56 * 1024 * 1024

EVEN_MAIN = 3 * 1024 + 3 * 1024 + 1024
DA_Q_BLK, DA_K_BLK, DA_V_BLK = 0, 8, 16
GDN_Q_BLK, GDN_K_BLK, GDN_V_BLK, GDN_Z_BLK = 24, 32, 40, 48
ODD_MAIN = 3 * FOX_HEADS * FOX_HEAD_DIM
FOX_K_BLK, FOX_V_BLK = 16, 32

PROJ_TM, PROJ_TN = 1024, 1024
MLP_TM, MLP_TF = 1024, 512
OUT_TM, OUT_TN = 2048, 512
GATE_TM = 256
FOX_TQ, FOX_TK_WIDE = 256, 256
DA_TQ, DA_TK_WIDE = 256, 256
N_CHUNKS = SEQ // GDN_CHUNK
GDN_GROUP = 256
CHUNKS_PER_GROUP = GDN_GROUP // GDN_CHUNK
N_GROUPS = SEQ // GDN_GROUP
GDN_INTERLEAVE = 8
GDN_HEADS_PER_STEP = 2

LOG2E = math.log2(math.e)
DA_QK_SCALE = DA_QK_DIM ** -0.5 * LOG2E
FOX_QK_SCALE = FOX_HEAD_DIM ** -0.5 * LOG2E

NT_DIMS = (((1,), (1,)), ((), ()))
TN_DIMS = (((0,), (0,)), ((), ()))


def _params(*sem):
    return pltpu.CompilerParams(dimension_semantics=sem, vmem_limit_bytes=VMEM_LIMIT)


def _rms_rows(x, g, eps):
    ms = jnp.mean(x * x, axis=-1, keepdims=True)
    return x * lax.rsqrt(ms + eps) * g


def _dot_bf16(a, b, dims=None):
    a = a.astype(BF16)
    b = b.astype(BF16)
    if dims is None:
        return jnp.dot(a, b, preferred_element_type=F32)
    return lax.dot_general(a, b, dims, preferred_element_type=F32)


def _dot_f32(a, b):
    return jnp.dot(a, b, precision=lax.Precision.HIGHEST, preferred_element_type=F32)


def _softplus(x):
    return jnp.maximum(x, 0.0) + jnp.log1p(jnp.exp(-jnp.abs(x)))


def _sigmoid(x):
    return 1.0 / (1.0 + jnp.exp(-x))


def _proj_kernel(x_ref, g_ref, w_ref, cs_ref, ws_ref, o_ref, os_ref, xn_ref):
    @pl.when(pl.program_id(1) == 0)
    def _():
        xn = _rms_rows(x_ref[...], g_ref[...], NORM_EPS).astype(BF16)
        xn_ref[...] = xn
        os_ref[...] = lax.dot_general(xn, ws_ref[...].astype(BF16), NT_DIMS, preferred_element_type=F32)

    y = lax.dot_general(xn_ref[...], w_ref[...].astype(BF16), NT_DIMS, preferred_element_type=F32)
    o_ref[...] = (y * cs_ref[...]).astype(o_ref.dtype)


def _in_proj(h, g, wt_stack, idx, n, col_scale, wt_side):
    grid = (TOKENS // PROJ_TM, n // PROJ_TN)
    return pl.pallas_call(
        _proj_kernel,
        grid=grid,
        in_specs=[
            pl.BlockSpec((PROJ_TM, D_MODEL), lambda i, j: (i, 0)),
            pl.BlockSpec((1, D_MODEL), lambda i, j: (0, 0)),
            pl.BlockSpec((None, PROJ_TN, D_MODEL), lambda i, j: (idx, j, 0)),
            pl.BlockSpec((1, PROJ_TN), lambda i, j: (0, j)),
            pl.BlockSpec((LANES, D_MODEL), lambda i, j: (0, 0)),
        ],
        out_specs=[
            pl.BlockSpec((PROJ_TM, PROJ_TN), lambda i, j: (i, j)),
            pl.BlockSpec((PROJ_TM, LANES), lambda i, j: (i, 0)),
        ],
        out_shape=[
            jax.ShapeDtypeStruct((TOKENS, n), BF16),
            jax.ShapeDtypeStruct((TOKENS, LANES), F32),
        ],
        scratch_shapes=[pltpu.VMEM((PROJ_TM, D_MODEL), BF16)],
        compiler_params=_params("arbitrary", "arbitrary"),
        name="in_proj",
    )(h, g, wt_stack, col_scale, wt_side)


def _same_chunk(r, c, chunk):
    return jnp.bitwise_xor(r, c) < chunk


def _tri_ones(n, chunk):
    r = lax.broadcasted_iota(jnp.int32, (n, n), 0)
    c = lax.broadcasted_iota(jnp.int32, (n, n), 1)
    return jnp.where((c <= r) & _same_chunk(r, c, chunk), 1.0, 0.0).astype(F32)


def _fox_gate_kernel(s_ref, bf_ref, c_ref, carry_ref):
    @pl.when(pl.program_id(1) == 0)
    def _():
        carry_ref[...] = jnp.zeros_like(carry_ref)

    x = s_ref[...] + bf_ref[...]
    log_f = jnp.minimum(x, 0.0) - jnp.log1p(jnp.exp(-jnp.abs(x)))
    c = _dot_f32(_tri_ones(GATE_TM, GATE_TM), log_f) + carry_ref[...]
    c_ref[...] = c
    carry_ref[...] = c[GATE_TM - 1:GATE_TM, :]


def _fox_gates(side, b_f_lanes):
    return pl.pallas_call(
        _fox_gate_kernel,
        grid=(BATCH, SEQ // GATE_TM),
        in_specs=[
            pl.BlockSpec((GATE_TM, LANES), lambda b, t: (b * (SEQ // GATE_TM) + t, 0)),
            pl.BlockSpec((1, LANES), lambda b, t: (0, 0)),
        ],
        out_specs=pl.BlockSpec((GATE_TM, LANES), lambda b, t: (b * (SEQ // GATE_TM) + t, 0)),
        out_shape=jax.ShapeDtypeStruct((TOKENS, LANES), F32),
        scratch_shapes=[pltpu.VMEM((1, LANES), F32)],
        compiler_params=_params("arbitrary", "arbitrary"),
        name="fox_gates",
    )(side, b_f_lanes)


def _gdn_gate_kernel(s_ref, alog_ref, dtb_ref, o_ref):
    x = s_ref[...]
    log_decay = -jnp.exp(alog_ref[...]) * _softplus(x + dtb_ref[...])
    g = _dot_f32(_tri_ones(GATE_TM, GDN_CHUNK), log_decay)
    lane = lax.broadcasted_iota(jnp.int32, x.shape, 1)
    o_ref[...] = jnp.where(lane < GDN_HEADS, g, _sigmoid(x))


def _gdn_gates(side, a_log_lanes, dt_bias_lanes):
    return pl.pallas_call(
        _gdn_gate_kernel,
        grid=(TOKENS // GATE_TM,),
        in_specs=[
            pl.BlockSpec((GATE_TM, LANES), lambda t: (t, 0)),
            pl.BlockSpec((1, LANES), lambda t: (0, 0)),
            pl.BlockSpec((1, LANES), lambda t: (0, 0)),
        ],
        out_specs=pl.BlockSpec((GATE_TM, LANES), lambda t: (t, 0)),
        out_shape=jax.ShapeDtypeStruct((TOKENS, LANES), F32),
        compiler_params=_params("arbitrary"),
        name="gdn_gates",
    )(side, a_log_lanes, dt_bias_lanes)


def _lane_pick(tile, lane_idx):
    lane = lax.broadcasted_iota(jnp.int32, tile.shape, 1)
    return jnp.sum(jnp.where(lane == lane_idx, tile, 0.0), axis=1, keepdims=True)


def _flash_static(q_of, k_ref, v_ref, cq_of, ck_of, emit, tq, tk_wide):
    for i in range(SEQ // tq):
        q = q_of(i)
        rows = q.shape[0]
        cq = cq_of(i)
        blocks, pos = [], 0
        while pos < i * tq:
            width = tk_wide if pos + tk_wide <= i * tq else tq
            blocks.append((pos, width, False))
            pos += width
        blocks.append((i * tq, tq, True))
        m = l = acc = None
        for j, (k0, width, diagonal) in enumerate(blocks):
            k = k_ref[k0:k0 + width, :]
            v = v_ref[k0:k0 + width, :]
            a = lax.dot_general(q, k, NT_DIMS, preferred_element_type=F32) - ck_of(k0, width)
            if diagonal:
                qpos = lax.broadcasted_iota(jnp.int32, (rows, width), 0) & (tq - 1)
                kpos = lax.broadcasted_iota(jnp.int32, (rows, width), 1)
                a = jnp.where(qpos >= kpos, a, -jnp.inf)
            m_blk = jnp.max(a, axis=1, keepdims=True) + cq
            if j == 0:
                m = m_blk
                p = jnp.exp2(a - (m - cq))
                l = jnp.sum(p, axis=1, keepdims=True)
                acc = jnp.dot(p.astype(BF16), v, preferred_element_type=F32)
            else:
                m_new = jnp.maximum(m, m_blk)
                alpha = jnp.exp2(m - m_new)
                p = jnp.exp2(a - (m_new - cq))
                l = alpha * l + jnp.sum(p, axis=1, keepdims=True)
                acc = alpha * acc + jnp.dot(p.astype(BF16), v, preferred_element_type=F32)
                m = m_new
        emit(i, acc / l)


def _fox_attn_kernel(q_ref, k_ref, v_ref, ctm_ref, chm_ref, o_ref):
    h = pl.program_id(1)

    def q_of(i):
        return q_ref[i * FOX_TQ:(i + 1) * FOX_TQ, :]

    def cq_of(i):
        return _lane_pick(ctm_ref[i * FOX_TQ:(i + 1) * FOX_TQ, :], h) * LOG2E

    def ck_of(k0, width):
        return chm_ref[pl.ds(h, 1), k0:k0 + width] * LOG2E

    def emit(i, o):
        o_ref[i * FOX_TQ:(i + 1) * FOX_TQ, :] = o.astype(o_ref.dtype)

    _flash_static(q_of, k_ref, v_ref, cq_of, ck_of, emit, FOX_TQ, FOX_TK_WIDE)


def _fox_attention(qkv, c_tm, c_hm):
    return pl.pallas_call(
        _fox_attn_kernel,
        grid=(BATCH, FOX_HEADS),
        in_specs=[
            pl.BlockSpec((SEQ, LANES), lambda b, h: (b, h)),
            pl.BlockSpec((SEQ, LANES), lambda b, h: (b, FOX_K_BLK + h)),
            pl.BlockSpec((SEQ, LANES), lambda b, h: (b, FOX_V_BLK + h)),
            pl.BlockSpec((SEQ, LANES), lambda b, h: (b, 0)),
            pl.BlockSpec((None, FOX_HEADS, SEQ), lambda b, h: (b, 0, 0)),
        ],
        out_specs=pl.BlockSpec((SEQ, LANES), lambda b, h: (b, h)),
        out_shape=jax.ShapeDtypeStruct((TOKENS, FOX_HEADS * FOX_HEAD_DIM), BF16),
        compiler_params=_params("arbitrary", "arbitrary"),
        name="fox_attention",
    )(qkv, qkv, qkv, c_tm, c_hm)


def _da_attn_kernel(slope_ref, q_ref, k_ref, v_ref, lam_ref, norm_ref, o_ref, *, lam_init):
    h = pl.program_id(1)
    c_per_pos = -slope_ref[h] * LOG2E

    lv = lam_ref[...]
    lam = (jnp.exp(jnp.sum(lv[0:1] * lv[1:2], axis=1, keepdims=True))
           - jnp.exp(jnp.sum(lv[2:3] * lv[3:4], axis=1, keepdims=True)) + lam_init)

    def q_of(i):
        q = q_ref[i * DA_TQ:(i + 1) * DA_TQ, :]
        lane = lax.broadcasted_iota(jnp.int32, q.shape, 1)
        zero = jnp.zeros_like(q)
        return jnp.concatenate([jnp.where(lane < DA_QK_DIM, q, zero),
                                jnp.where(lane >= DA_QK_DIM, q, zero)], axis=0)

    def cq_of(i):
        pos = i * DA_TQ + (lax.broadcasted_iota(jnp.int32, (2 * DA_TQ, 1), 0) & (DA_TQ - 1))
        return c_per_pos * pos.astype(F32)

    def ck_of(k0, width):
        pos = k0 + lax.broadcasted_iota(jnp.int32, (1, width), 1)
        return c_per_pos * pos.astype(F32)

    def emit(i, o2):
        o = o2[:DA_TQ] - lam * o2[DA_TQ:]
        o = _rms_rows(o, norm_ref[...], NORM_EPS) * (1.0 - lam_init)
        o_ref[i * DA_TQ:(i + 1) * DA_TQ, :] = o.astype(o_ref.dtype)

    _flash_static(q_of, k_ref, v_ref, cq_of, ck_of, emit, DA_TQ, DA_TK_WIDE)


def _da_attention(proj, slopes, lam_vecs, da_norm, lam_init):
    return pl.pallas_call(
        functools.partial(_da_attn_kernel, lam_init=lam_init),
        grid=(BATCH, DA_HEADS),
        in_specs=[
            pl.BlockSpec(memory_space=pltpu.SMEM),
            pl.BlockSpec((SEQ, LANES), lambda b, h: (b, DA_Q_BLK + h)),
            pl.BlockSpec((SEQ, LANES), lambda b, h: (b, DA_K_BLK + h)),
            pl.BlockSpec((SEQ, LANES), lambda b, h: (b, DA_V_BLK + h)),
            pl.BlockSpec((4, DA_QK_DIM), lambda b, h: (0, 0)),
            pl.BlockSpec((1, DA_V_DIM), lambda b, h: (0, 0)),
        ],
        out_specs=pl.BlockSpec((SEQ, LANES), lambda b, h: (b, h)),
        out_shape=jax.ShapeDtypeStruct((TOKENS, DA_HEADS * DA_V_DIM), BF16),
        compiler_params=_params("arbitrary", "arbitrary"),
        name="da_attention",
    )(slopes, proj, proj, proj, lam_vecs, da_norm)


def _conv_silu(x, w):
    x = x.astype(F32)
    row = lax.broadcasted_iota(jnp.int32, x.shape, 0)
    y = x * w[CONV_WIDTH - 1:CONV_WIDTH]
    for back in range(1, CONV_WIDTH):
        shifted = jnp.where(row >= back, pltpu.roll(x, back, axis=0), 0.0)
        y = y + shifted * w[CONV_WIDTH - 1 - back:CONV_WIDTH - back]
    return y * _sigmoid(y)


def _l2norm_rows(t):
    return t * lax.rsqrt(jnp.sum(t * t, axis=-1, keepdims=True) + L2_EPS)


def _gdn_kernel(q_ref, k_ref, v_ref, z_ref, wq_ref, wk_ref, wv_ref, gtm_ref, ghm_ref, norm_ref,
                o_ref, qs_ref, ks_ref, vs_ref, qp_ref, o0_ref, mt_ref, nt_ref, egl_ref, os_ref):
    c = GDN_CHUNK
    grp = GDN_GROUP
    heads = range(GDN_HEADS_PER_STEP)
    head0 = pl.program_id(1) * GDN_HEADS_PER_STEP

    ii = lax.broadcasted_iota(jnp.int32, (grp, grp), 0)
    jj = lax.broadcasted_iota(jnp.int32, (grp, grp), 1)
    sib = jnp.bitwise_xor(ii, jj)
    same = sib < c
    incl = same & (ii >= jj)
    strict = same & (ii > jj)
    last_of_chunk = jj == (ii | (c - 1))
    eye = jnp.where(ii == jj, 1.0, 0.0).astype(F32)

    for hh in heads:
        h = head0 + hh
        cols = slice(hh * LANES, (hh + 1) * LANES)
        qs_ref[hh] = _l2norm_rows(_conv_silu(q_ref[:, cols], wq_ref[:, cols])) * (GDN_K_DIM ** -0.5)
        ks_ref[hh] = _l2norm_rows(_conv_silu(k_ref[:, cols], wk_ref[:, cols]))
        vs_ref[hh] = _conv_silu(v_ref[:, cols], wv_ref[:, cols])

        for g0 in range(0, N_GROUPS, GDN_INTERLEAVE):
            lanes = range(GDN_INTERLEAVE)
            rows = [pl.ds((g0 + x) * grp, grp) for x in lanes]
            q = [qs_ref[hh, r, :] for r in rows]
            k = [ks_ref[hh, r, :] for r in rows]
            v = [vs_ref[hh, r, :] for r in rows]
            gates = [gtm_ref[r, :] for r in rows]
            g_col = [_lane_pick(gt, h) for gt in gates]
            beta = [_lane_pick(gt, GDN_HEADS + h) for gt in gates]
            g_row = [ghm_ref[h, pl.ds(g0 + x, 1), :] for x in lanes]
            decay = [jnp.exp(jnp.where(incl, gc - gr, -jnp.inf)) for gc, gr in zip(g_col, g_row)]
            eg = [jnp.exp(gc) for gc in g_col]
            k_beta = [kx * bx for kx, bx in zip(k, beta)]
            a = [jnp.where(strict, _dot_bf16(kb, kx, NT_DIMS) * dc, 0.0)
                 for kb, kx, dc in zip(k_beta, k, decay)]
            t_inv = [eye - jnp.where(sib == 1, ax, 0.0) for ax in a]
            s = 2
            while s < c:
                coupling = (sib >= s) & (sib < 2 * s)
                at = [_dot_bf16(jnp.where(coupling, ax, 0.0), tx) for ax, tx in zip(a, t_inv)]
                t_inv = [tx - _dot_bf16(tx, atx) for tx, atx in zip(t_inv, at)]
                s *= 2
            wu = [_dot_bf16(tx, jnp.concatenate([kb * ex, vx * bx], axis=1))
                  for tx, kb, ex, vx, bx in zip(t_inv, k_beta, eg, v, beta)]
            intra = [_dot_bf16(qx, kx, NT_DIMS) * dc for qx, kx, dc in zip(q, k, decay)]
            iwu = [_dot_bf16(ix, wx) for ix, wx in zip(intra, wu)]
            for x in lanes:
                qp_ref[hh, rows[x], :] = (q[x] * eg[x] - iwu[x][:, :GDN_K_DIM]).astype(BF16)
                o0_ref[hh, rows[x], :] = iwu[x][:, GDN_K_DIM:]
            for x in lanes:
                gl_col = jnp.sum(jnp.where(last_of_chunk, g_row[x], 0.0), axis=1, keepdims=True)
                k_dec = k[x] * jnp.exp(gl_col - g_col[x])
                for ci in range(CHUNKS_PER_GROUP):
                    n = (g0 + x) * CHUNKS_PER_GROUP + ci
                    zt = _dot_bf16(wu[x][ci * c:(ci + 1) * c, :], k_dec[ci * c:(ci + 1) * c, :], TN_DIMS)
                    mt_ref[hh, pl.ds(n * GDN_K_DIM, GDN_K_DIM), :] = zt[:GDN_K_DIM].astype(BF16)
                    nt_ref[hh, pl.ds(n * GDN_V_DIM, GDN_V_DIM), :] = zt[GDN_K_DIM:]
                    gl = g_row[x][:, (ci + 1) * c - 1:(ci + 1) * c]
                    egl_ref[hh, pl.ds(n, 1), :] = jnp.broadcast_to(jnp.exp(gl), (1, LANES))

    def step(n, states):
        r0 = pl.multiple_of(n * c, c)
        m0 = pl.multiple_of(n * GDN_K_DIM, GDN_K_DIM)
        new_states = []
        for hh, st in zip(heads, states):
            st_b = st.astype(BF16)
            os_ref[hh, pl.ds(r0, c), :] = (
                lax.dot_general(qp_ref[hh, pl.ds(r0, c), :], st_b, NT_DIMS, preferred_element_type=F32)
                + o0_ref[hh, pl.ds(r0, c), :])
            new_states.append(
                st * egl_ref[hh, pl.ds(n, 1), :]
                - jnp.dot(st_b, mt_ref[hh, pl.ds(m0, GDN_K_DIM), :], preferred_element_type=F32)
                + nt_ref[hh, pl.ds(m0, GDN_V_DIM), :])
        return tuple(new_states)

    zero_state = jnp.zeros((GDN_V_DIM, GDN_K_DIM), F32)
    lax.fori_loop(0, N_CHUNKS, step, tuple(zero_state for _ in heads), unroll=2)

    for hh in heads:
        cols = slice(hh * LANES, (hh + 1) * LANES)
        z = z_ref[:, cols].astype(F32)
        o = _rms_rows(os_ref[hh], norm_ref[...], NORM_EPS) * (z * _sigmoid(z))
        o_ref[:, cols] = o.astype(o_ref.dtype)


def _gdn(proj, conv_w, gates_tm, g_hm, gdn_norm):
    hps = GDN_HEADS_PER_STEP
    width = hps * LANES
    seq_blk = lambda off: pl.BlockSpec((SEQ, width), lambda b, p: (b, off // hps + p))
    conv_blk = lambda off: pl.BlockSpec((CONV_WIDTH, width), lambda b, p: (0, off // hps + p))
    per_head = lambda rows, dtype: pltpu.VMEM((hps, rows, LANES), dtype)
    return pl.pallas_call(
        _gdn_kernel,
        grid=(BATCH, GDN_HEADS // hps),
        in_specs=[
            seq_blk(GDN_Q_BLK), seq_blk(GDN_K_BLK), seq_blk(GDN_V_BLK), seq_blk(GDN_Z_BLK),
            conv_blk(0), conv_blk(GDN_HEADS), conv_blk(2 * GDN_HEADS),
            pl.BlockSpec((SEQ, LANES), lambda b, p: (b, 0)),
            pl.BlockSpec((GDN_HEADS, N_GROUPS, GDN_GROUP), lambda b, p: (0, b, 0)),
            pl.BlockSpec((1, GDN_V_DIM), lambda b, p: (0, 0)),
        ],
        out_specs=pl.BlockSpec((SEQ, width), lambda b, p: (b, p)),
        out_shape=jax.ShapeDtypeStruct((TOKENS, GDN_HEADS * GDN_V_DIM), BF16),
        scratch_shapes=[
            per_head(SEQ, F32),
            per_head(SEQ, F32),
            per_head(SEQ, F32),
            per_head(SEQ, BF16),
            per_head(SEQ, F32),
            per_head(N_CHUNKS * GDN_K_DIM, BF16),
            per_head(N_CHUNKS * GDN_V_DIM, F32),
            per_head(N_CHUNKS, F32),
            per_head(SEQ, F32),
        ],
        compiler_params=_params("arbitrary", "arbitrary"),
        name="gdn",
    )(proj, proj, proj, proj, conv_w, conv_w, conv_w, gates_tm, g_hm, gdn_norm)


def _out_proj_kernel(xa_ref, xb_ref, wa_ref, wb_ref, h_ref, o_ref):
    o_ref[...] = (h_ref[...]
                  + jnp.dot(xa_ref[...], wa_ref[...].astype(BF16), preferred_element_type=F32)
                  + jnp.dot(xb_ref[...], wb_ref[...].astype(BF16), preferred_element_type=F32))


def _out_proj(xa, xb_spec_col, xb, w_stack, idx, h):
    half = w_stack.shape[1] // 2
    return pl.pallas_call(
        _out_proj_kernel,
        grid=(TOKENS // OUT_TM, D_MODEL // OUT_TN),
        in_specs=[
            pl.BlockSpec((OUT_TM, half), lambda i, j: (i, 0)),
            pl.BlockSpec((OUT_TM, half), lambda i, j: (i, xb_spec_col)),
            pl.BlockSpec((None, half, OUT_TN), lambda i, j: (idx, 0, j)),
            pl.BlockSpec((None, half, OUT_TN), lambda i, j: (idx, 1, j)),
            pl.BlockSpec((OUT_TM, OUT_TN), lambda i, j: (i, j)),
        ],
        out_specs=pl.BlockSpec((OUT_TM, OUT_TN), lambda i, j: (i, j)),
        out_shape=jax.ShapeDtypeStruct((TOKENS, D_MODEL), F32),
        compiler_params=_params("arbitrary", "arbitrary"),
        name="out_proj",
    )(xa, xb, w_stack, w_stack, h)


def _mlp_kernel(h_ref, g_ref, wu_ref, wd_ref, gf_ref, o_ref, xn_ref, *, final_norm):
    f = pl.program_id(1)

    @pl.when(f == 0)
    def _():
        x = h_ref[...]
        xn_ref[...] = _rms_rows(x, g_ref[...], NORM_EPS).astype(BF16)
        o_ref[...] = x

    a = jnp.maximum(jnp.dot(xn_ref[...], wu_ref[...].astype(BF16), preferred_element_type=F32), 0.0)
    o_ref[...] += jnp.dot((a * a).astype(BF16), wd_ref[...].astype(BF16), preferred_element_type=F32)

    if final_norm:
        @pl.when(f == pl.num_programs(1) - 1)
        def _():
            o_ref[...] = _rms_rows(o_ref[...], gf_ref[...], NORM_EPS)


def _mlp(h, g, w_up, w_down, layer, g_final, final_norm):
    return pl.pallas_call(
        functools.partial(_mlp_kernel, final_norm=final_norm),
        grid=(TOKENS // MLP_TM, D_FF // MLP_TF),
        in_specs=[
            pl.BlockSpec((MLP_TM, D_MODEL), lambda i, f: (i, 0)),
            pl.BlockSpec((1, D_MODEL), lambda i, f: (0, 0)),
            pl.BlockSpec((None, D_MODEL, MLP_TF), lambda i, f: (layer, 0, f)),
            pl.BlockSpec((None, MLP_TF, D_MODEL), lambda i, f: (layer, f, 0)),
            pl.BlockSpec((1, D_MODEL), lambda i, f: (0, 0)),
        ],
        out_specs=pl.BlockSpec((MLP_TM, D_MODEL), lambda i, f: (i, 0)),
        out_shape=jax.ShapeDtypeStruct((TOKENS, D_MODEL), F32),
        scratch_shapes=[pltpu.VMEM((MLP_TM, D_MODEL), BF16)],
        compiler_params=_params("arbitrary", "arbitrary"),
        name="mlp",
    )(h, g, w_up, w_down, g_final)


def _lanes(vec, offset=0):
    return jnp.zeros((1, LANES), F32).at[0, offset:offset + vec.shape[0]].set(vec.astype(F32))


def _side_weight(wt):
    return jnp.pad(wt, ((0, LANES - wt.shape[0]), (0, 0)))


def _q_col_scale(n, q_cols, scale):
    return jnp.where(jnp.arange(n) < q_cols, scale, 1.0).astype(F32).reshape(1, n)


def _even_mixer(h, g_mix, w_in, idx, conv_w, lam_vecs, da_norm, a_log, dt_bias, gdn_norm, w_out, layer):
    wt_in = jnp.swapaxes(w_in, 1, 2)
    proj, side = _in_proj(h, g_mix, wt_in, idx, EVEN_MAIN,
                          _q_col_scale(EVEN_MAIN, DA_HEADS * 2 * DA_QK_DIM, DA_QK_SCALE),
                          _side_weight(wt_in[idx, EVEN_MAIN:, :]))
    lam_init = 0.8 - 0.6 * math.exp(-0.3 * layer)
    slopes = 2.0 ** (-8.0 * jnp.arange(1, DA_HEADS + 1, dtype=F32) / DA_HEADS)
    o_a = _da_attention(proj, slopes, lam_vecs, da_norm.reshape(1, DA_V_DIM), lam_init)

    gates_tm = _gdn_gates(side, _lanes(a_log), _lanes(dt_bias))
    g_hm = gates_tm[:, :GDN_HEADS].T.reshape(GDN_HEADS, BATCH * N_GROUPS, GDN_GROUP)
    o_b = _gdn(proj, conv_w, gates_tm, g_hm, gdn_norm.reshape(1, GDN_V_DIM))
    return _out_proj(o_a, 0, o_b, w_out, idx, h)


def _odd_mixer(h, g_mix, w_in, idx, b_f, w_out):
    wt_in = jnp.swapaxes(w_in, 1, 2)
    qkv, side = _in_proj(h, g_mix, wt_in, idx, ODD_MAIN,
                         _q_col_scale(ODD_MAIN, FOX_HEADS * FOX_HEAD_DIM, FOX_QK_SCALE),
                         _side_weight(wt_in[idx, ODD_MAIN:, :]))
    c_tm = _fox_gates(side, _lanes(b_f))
    c_hm = c_tm.reshape(BATCH, SEQ, LANES)[:, :, :FOX_HEADS].transpose(0, 2, 1)
    o = _fox_attention(qkv, c_tm, c_hm)
    return _out_proj(o, 1, o, w_out, idx, h)


def kernel(x, norm_mix, norm_mlp, norm_final, w_in_even, conv_w, lam_q1, lam_k1, lam_q2, lam_k2,
           da_norm, gdn_a_log, gdn_dt_bias, gdn_norm, w_out_even, w_in_odd, fox_b_f, w_out_odd,
           w_up, w_down):
    h = x.reshape(TOKENS, D_MODEL)
    g_final = norm_final.reshape(1, D_MODEL)
    for layer in range(DEPTH):
        i = layer // 2
        g_mix = norm_mix[layer].reshape(1, D_MODEL)
        if layer % 2 == 0:
            lam_vecs = jnp.stack([lam_q1[i], lam_k1[i], lam_q2[i], lam_k2[i]]).astype(F32)
            h = _even_mixer(h, g_mix, w_in_even, i, conv_w[i], lam_vecs, da_norm[i], gdn_a_log[i],
                            gdn_dt_bias[i], gdn_norm[i], w_out_even, layer)
        else:
            h = _odd_mixer(h, g_mix, w_in_odd, i, fox_b_f[i], w_out_odd)
        h = _mlp(h, norm_mlp[layer].reshape(1, D_MODEL), w_up, w_down, layer, g_final, layer == DEPTH - 1)
    return h.reshape(BATCH, SEQ, D_MODEL)
```

```python
import functools
import math

import jax
import jax.numpy as jnp
from jax import lax
from jax.experimental import pallas as pl
from jax.experimental.pallas import tpu as pltpu

F32 = jnp.float32
BF16 = jnp.bfloat16

D_MODEL = 2048
BATCH = 4
SEQ = 2048
DEPTH = 4
TOKENS = BATCH * SEQ

DA_HEADS = 8
DA_QK_DIM = 64
DA_V_DIM = 128
GDN_HEADS = 8
GDN_K_DIM = 128
GDN_V_DIM = 128
GDN_CHUNK = 64
CONV_WIDTH = 4
FOX_HEADS = 16
FOX_HEAD_DIM = 128
D_FF = 4 * D_MODEL
NORM_EPS = 1e-6
L2_EPS = 1e-6

LANES = 128
VMEM_LIMIT = 56 * 1024 * 1024

EVEN_MAIN = 3 * 1024 + 3 * 1024 + 1024
DA_Q_BLK, DA_K_BLK, DA_V_BLK = 0, 8, 16
GDN_Q_BLK, GDN_K_BLK, GDN_V_BLK, GDN_Z_BLK = 24, 32, 40, 48
ODD_MAIN = 3 * FOX_HEADS * FOX_HEAD_DIM
FOX_K_BLK, FOX_V_BLK = 16, 32

PROJ_TM, PROJ_TN = 1024, 1024
MLP_TM, MLP_TF = 1024, 512
OUT_TM, OUT_TN = 2048, 512
GATE_TM, GATE_SUB = 1024, 256
FOX_TQ, FOX_TK_WIDE = 256, 256
DA_TQ, DA_TK_WIDE = 256, 256
N_CHUNKS = SEQ // GDN_CHUNK
GDN_GROUP = 256
CHUNKS_PER_GROUP = GDN_GROUP // GDN_CHUNK
N_GROUPS = SEQ // GDN_GROUP
GDN_INTERLEAVE = 8
GDN_HEADS_PER_STEP = 2

LOG2E = math.log2(math.e)
DA_QK_SCALE = DA_QK_DIM ** -0.5 * LOG2E
FOX_QK_SCALE = FOX_HEAD_DIM ** -0.5 * LOG2E

NT_DIMS = (((1,), (1,)), ((), ()))
TN_DIMS = (((0,), (0,)), ((), ()))


def _params(*sem):
    return pltpu.CompilerParams(dimension_semantics=sem, vmem_limit_bytes=VMEM_LIMIT)


def _rms_rows(x, g, eps):
    ms = jnp.mean(x * x, axis=-1, keepdims=True)
    return x * lax.rsqrt(ms + eps) * g


def _dot_bf16(a, b, dims=None):
    a = a.astype(BF16)
    b = b.astype(BF16)
    if dims is None:
        return jnp.dot(a, b, preferred_element_type=F32)
    return lax.dot_general(a, b, dims, preferred_element_type=F32)


def _dot_ones_matrix(ones, x):
    t = ones.astype(BF16)
    hi = x.astype(BF16)
    rest = x - hi.astype(F32)
    mid = rest.astype(BF16)
    lo = (rest - mid.astype(F32)).astype(BF16)
    return (jnp.dot(t, hi, preferred_element_type=F32) + jnp.dot(t, mid, preferred_element_type=F32)
            + jnp.dot(t, lo, preferred_element_type=F32))


def _softplus(x):
    return jnp.maximum(x, 0.0) + jnp.log1p(jnp.exp(-jnp.abs(x)))


def _sigmoid(x):
    return 1.0 / (1.0 + jnp.exp(-x))


def _proj_kernel(x_ref, g_ref, w_ref, cs_ref, ws_ref, o_ref, os_ref, xn_ref):
    @pl.when(pl.program_id(1) == 0)
    def _():
        xn = _rms_rows(x_ref[...], g_ref[...], NORM_EPS).astype(BF16)
        xn_ref[...] = xn
        os_ref[...] = lax.dot_general(xn, ws_ref[...].astype(BF16), NT_DIMS, preferred_element_type=F32)

    y = lax.dot_general(xn_ref[...], w_ref[...].astype(BF16), NT_DIMS, preferred_element_type=F32)
    o_ref[...] = (y * cs_ref[...]).astype(o_ref.dtype)


def _in_proj(h, g, wt_stack, idx, n, col_scale, wt_side):
    grid = (TOKENS // PROJ_TM, n // PROJ_TN)
    return pl.pallas_call(
        _proj_kernel,
        grid=grid,
        in_specs=[
            pl.BlockSpec((PROJ_TM, D_MODEL), lambda i, j: (i, 0)),
            pl.BlockSpec((1, D_MODEL), lambda i, j: (0, 0)),
            pl.BlockSpec((None, PROJ_TN, D_MODEL), lambda i, j: (idx, j, 0)),
            pl.BlockSpec((1, PROJ_TN), lambda i, j: (0, j)),
            pl.BlockSpec((LANES, D_MODEL), lambda i, j: (0, 0)),
        ],
        out_specs=[
            pl.BlockSpec((PROJ_TM, PROJ_TN), lambda i, j: (i, j)),
            pl.BlockSpec((PROJ_TM, LANES), lambda i, j: (i, 0)),
        ],
        out_shape=[
            jax.ShapeDtypeStruct((TOKENS, n), BF16),
            jax.ShapeDtypeStruct((TOKENS, LANES), F32),
        ],
        scratch_shapes=[pltpu.VMEM((PROJ_TM, D_MODEL), BF16)],
        compiler_params=_params("arbitrary", "arbitrary"),
        name="in_proj",
    )(h, g, wt_stack, col_scale, wt_side)


def _same_chunk(r, c, chunk):
    return jnp.bitwise_xor(r, c) < chunk


def _tri_ones(n, chunk):
    r = lax.broadcasted_iota(jnp.int32, (n, n), 0)
    c = lax.broadcasted_iota(jnp.int32, (n, n), 1)
    return jnp.where((c <= r) & _same_chunk(r, c, chunk), 1.0, 0.0).astype(F32)


def _fox_gate_kernel(s_ref, bf_ref, c_ref, carry_ref):
    @pl.when(pl.program_id(1) == 0)
    def _():
        carry_ref[...] = jnp.zeros_like(carry_ref)

    tri = _tri_ones(GATE_SUB, GATE_SUB)
    carry = carry_ref[...]
    for sub in range(GATE_TM // GATE_SUB):
        rows = slice(sub * GATE_SUB, (sub + 1) * GATE_SUB)
        x = s_ref[rows, :] + bf_ref[...]
        log_f = jnp.minimum(x, 0.0) - jnp.log1p(jnp.exp(-jnp.abs(x)))
        c = _dot_ones_matrix(tri, log_f) + carry
        c_ref[rows, :] = c
        carry = c[GATE_SUB - 1:GATE_SUB, :]
    carry_ref[...] = carry


def _fox_gates(side, b_f_lanes):
    return pl.pallas_call(
        _fox_gate_kernel,
        grid=(BATCH, SEQ // GATE_TM),
        in_specs=[
            pl.BlockSpec((GATE_TM, LANES), lambda b, t: (b * (SEQ // GATE_TM) + t, 0)),
            pl.BlockSpec((1, LANES), lambda b, t: (0, 0)),
        ],
        out_specs=pl.BlockSpec((GATE_TM, LANES), lambda b, t: (b * (SEQ // GATE_TM) + t, 0)),
        out_shape=jax.ShapeDtypeStruct((TOKENS, LANES), F32),
        scratch_shapes=[pltpu.VMEM((1, LANES), F32)],
        compiler_params=_params("arbitrary", "arbitrary"),
        name="fox_gates",
    )(side, b_f_lanes)


def _gdn_gate_kernel(s_ref, alog_ref, dtb_ref, o_ref):
    tri = _tri_ones(GATE_SUB, GDN_CHUNK)
    lane = lax.broadcasted_iota(jnp.int32, (GATE_SUB, LANES), 1)
    for sub in range(GATE_TM // GATE_SUB):
        rows = slice(sub * GATE_SUB, (sub + 1) * GATE_SUB)
        x = s_ref[rows, :]
        log_decay = -jnp.exp(alog_ref[...]) * _softplus(x + dtb_ref[...])
        g = _dot_ones_matrix(tri, log_decay)
        o_ref[rows, :] = jnp.where(lane < GDN_HEADS, g, _sigmoid(x))


def _gdn_gates(side, a_log_lanes, dt_bias_lanes):
    return pl.pallas_call(
        _gdn_gate_kernel,
        grid=(TOKENS // GATE_TM,),
        in_specs=[
            pl.BlockSpec((GATE_TM, LANES), lambda t: (t, 0)),
            pl.BlockSpec((1, LANES), lambda t: (0, 0)),
            pl.BlockSpec((1, LANES), lambda t: (0, 0)),
        ],
        out_specs=pl.BlockSpec((GATE_TM, LANES), lambda t: (t, 0)),
        out_shape=jax.ShapeDtypeStruct((TOKENS, LANES), F32),
        compiler_params=_params("arbitrary"),
        name="gdn_gates",
    )(side, a_log_lanes, dt_bias_lanes)


def _lane_pick(tile, lane_idx):
    lane = lax.broadcasted_iota(jnp.int32, tile.shape, 1)
    return jnp.sum(jnp.where(lane == lane_idx, tile, 0.0), axis=1, keepdims=True)


def _flash_static(q_of, k_ref, v_ref, cq_of, ck_of, emit, tq, tk_wide):
    for i in range(SEQ // tq):
        q = q_of(i)
        rows = q.shape[0]
        cq = cq_of(i)
        blocks, pos = [], 0
        while pos < i * tq:
            width = tk_wide if pos + tk_wide <= i * tq else tq
            blocks.append((pos, width, False))
            pos += width
        blocks.append((i * tq, tq, True))
        m = l = acc = None
        for j, (k0, width, diagonal) in enumerate(blocks):
            k = k_ref[k0:k0 + width, :]
            v = v_ref[k0:k0 + width, :]
            a = lax.dot_general(q, k, NT_DIMS, preferred_element_type=F32) - ck_of(k0, width)
            if diagonal:
                qpos = lax.broadcasted_iota(jnp.int32, (rows, width), 0) & (tq - 1)
                kpos = lax.broadcasted_iota(jnp.int32, (rows, width), 1)
                a = jnp.where(qpos >= kpos, a, -jnp.inf)
            m_blk = jnp.max(a, axis=1, keepdims=True) + cq
            if j == 0:
                m = m_blk
                p = jnp.exp2(a - (m - cq))
                l = jnp.sum(p, axis=1, keepdims=True)
                acc = jnp.dot(p.astype(BF16), v, preferred_element_type=F32)
            else:
                m_new = jnp.maximum(m, m_blk)
                alpha = jnp.exp2(m - m_new)
                p = jnp.exp2(a - (m_new - cq))
                l = alpha * l + jnp.sum(p, axis=1, keepdims=True)
                acc = alpha * acc + jnp.dot(p.astype(BF16), v, preferred_element_type=F32)
                m = m_new
        emit(i, acc / l)


def _fox_attn_kernel(q_ref, k_ref, v_ref, ctm_ref, chm_ref, o_ref):
    h = pl.program_id(1)

    def q_of(i):
        return q_ref[i * FOX_TQ:(i + 1) * FOX_TQ, :]

    def cq_of(i):
        return _lane_pick(ctm_ref[i * FOX_TQ:(i + 1) * FOX_TQ, :], h) * LOG2E

    def ck_of(k0, width):
        return chm_ref[pl.ds(h, 1), k0:k0 + width] * LOG2E

    def emit(i, o):
        o_ref[i * FOX_TQ:(i + 1) * FOX_TQ, :] = o.astype(o_ref.dtype)

    _flash_static(q_of, k_ref, v_ref, cq_of, ck_of, emit, FOX_TQ, FOX_TK_WIDE)


def _fox_attention(qkv, c_tm, c_hm):
    return pl.pallas_call(
        _fox_attn_kernel,
        grid=(BATCH, FOX_HEADS),
        in_specs=[
            pl.BlockSpec((SEQ, LANES), lambda b, h: (b, h)),
            pl.BlockSpec((SEQ, LANES), lambda b, h: (b, FOX_K_BLK + h)),
            pl.BlockSpec((SEQ, LANES), lambda b, h: (b, FOX_V_BLK + h)),
            pl.BlockSpec((SEQ, LANES), lambda b, h: (b, 0)),
            pl.BlockSpec((None, FOX_HEADS, SEQ), lambda b, h: (b, 0, 0)),
        ],
        out_specs=pl.BlockSpec((SEQ, LANES), lambda b, h: (b, h)),
        out_shape=jax.ShapeDtypeStruct((TOKENS, FOX_HEADS * FOX_HEAD_DIM), BF16),
        compiler_params=_params("arbitrary", "arbitrary"),
        name="fox_attention",
    )(qkv, qkv, qkv, c_tm, c_hm)


def _da_attn_kernel(slope_ref, q_ref, k_ref, v_ref, lam_ref, norm_ref, o_ref, *, lam_init):
    h = pl.program_id(1)
    c_per_pos = -slope_ref[h] * LOG2E

    lv = lam_ref[...]
    lam = (jnp.exp(jnp.sum(lv[0:1] * lv[1:2], axis=1, keepdims=True))
           - jnp.exp(jnp.sum(lv[2:3] * lv[3:4], axis=1, keepdims=True)) + lam_init)

    def q_of(i):
        q = q_ref[i * DA_TQ:(i + 1) * DA_TQ, :]
        lane = lax.broadcasted_iota(jnp.int32, q.shape, 1)
        zero = jnp.zeros_like(q)
        return jnp.concatenate([jnp.where(lane < DA_QK_DIM, q, zero),
                                jnp.where(lane >= DA_QK_DIM, q, zero)], axis=0)

    def cq_of(i):
        pos = i * DA_TQ + (lax.broadcasted_iota(jnp.int32, (2 * DA_TQ, 1), 0) & (DA_TQ - 1))
        return c_per_pos * pos.astype(F32)

    def ck_of(k0, width):
        pos = k0 + lax.broadcasted_iota(jnp.int32, (1, width), 1)
        return c_per_pos * pos.astype(F32)

    def emit(i, o2):
        o = o2[:DA_TQ] - lam * o2[DA_TQ:]
        o = _rms_rows(o, norm_ref[...], NORM_EPS) * (1.0 - lam_init)
        o_ref[i * DA_TQ:(i + 1) * DA_TQ, :] = o.astype(o_ref.dtype)

    _flash_static(q_of, k_ref, v_ref, cq_of, ck_of, emit, DA_TQ, DA_TK_WIDE)


def _da_attention(proj, slopes, lam_vecs, da_norm, lam_init):
    return pl.pallas_call(
        functools.partial(_da_attn_kernel, lam_init=lam_init),
        grid=(BATCH, DA_HEADS),
        in_specs=[
            pl.BlockSpec(memory_space=pltpu.SMEM),
            pl.BlockSpec((SEQ, LANES), lambda b, h: (b, DA_Q_BLK + h)),
            pl.BlockSpec((SEQ, LANES), lambda b, h: (b, DA_K_BLK + h)),
            pl.BlockSpec((SEQ, LANES), lambda b, h: (b, DA_V_BLK + h)),
            pl.BlockSpec((4, DA_QK_DIM), lambda b, h: (0, 0)),
            pl.BlockSpec((1, DA_V_DIM), lambda b, h: (0, 0)),
        ],
        out_specs=pl.BlockSpec((SEQ, LANES), lambda b, h: (b, h)),
        out_shape=jax.ShapeDtypeStruct((TOKENS, DA_HEADS * DA_V_DIM), BF16),
        compiler_params=_params("arbitrary", "arbitrary"),
        name="da_attention",
    )(slopes, proj, proj, proj, lam_vecs, da_norm)


def _conv_silu(x, w):
    x = x.astype(F32)
    row = lax.broadcasted_iota(jnp.int32, x.shape, 0)
    y = x * w[CONV_WIDTH - 1:CONV_WIDTH]
    for back in range(1, CONV_WIDTH):
        shifted = jnp.where(row >= back, pltpu.roll(x, back, axis=0), 0.0)
        y = y + shifted * w[CONV_WIDTH - 1 - back:CONV_WIDTH - back]
    return y * _sigmoid(y)


def _l2norm_rows(t):
    return t * lax.rsqrt(jnp.sum(t * t, axis=-1, keepdims=True) + L2_EPS)


def _gdn_kernel(q_ref, k_ref, v_ref, z_ref, wq_ref, wk_ref, wv_ref, gtm_ref, ghm_ref, norm_ref,
                o_ref, qs_ref, ks_ref, vs_ref, qp_ref, o0_ref, mt_ref, nt_ref, egl_ref, os_ref):
    c = GDN_CHUNK
    grp = GDN_GROUP
    heads = range(GDN_HEADS_PER_STEP)
    head0 = pl.program_id(1) * GDN_HEADS_PER_STEP

    ii = lax.broadcasted_iota(jnp.int32, (grp, grp), 0)
    jj = lax.broadcasted_iota(jnp.int32, (grp, grp), 1)
    sib = jnp.bitwise_xor(ii, jj)
    same = sib < c
    incl = same & (ii >= jj)
    strict = same & (ii > jj)
    last_of_chunk = jj == (ii | (c - 1))
    eye = jnp.where(ii == jj, 1.0, 0.0).astype(F32)

    for hh in heads:
        h = head0 + hh
        cols = slice(hh * LANES, (hh + 1) * LANES)
        qs_ref[hh] = _l2norm_rows(_conv_silu(q_ref[:, cols], wq_ref[:, cols])) * (GDN_K_DIM ** -0.5)
        ks_ref[hh] = _l2norm_rows(_conv_silu(k_ref[:, cols], wk_ref[:, cols]))
        vs_ref[hh] = _conv_silu(v_ref[:, cols], wv_ref[:, cols])

        for g0 in range(0, N_GROUPS, GDN_INTERLEAVE):
            lanes = range(GDN_INTERLEAVE)
            rows = [pl.ds((g0 + x) * grp, grp) for x in lanes]
            q = [qs_ref[hh, r, :] for r in rows]
            k = [ks_ref[hh, r, :] for r in rows]
            v = [vs_ref[hh, r, :] for r in rows]
            gates = [gtm_ref[r, :] for r in rows]
            g_col = [_lane_pick(gt, h) for gt in gates]
            beta = [_lane_pick(gt, GDN_HEADS + h) for gt in gates]
            g_row = [ghm_ref[h, pl.ds(g0 + x, 1), :] for x in lanes]
            decay = [jnp.exp(jnp.where(incl, gc - gr, -jnp.inf)) for gc, gr in zip(g_col, g_row)]
            eg = [jnp.exp(gc) for gc in g_col]
            k_beta = [kx * bx for kx, bx in zip(k, beta)]
            a = [jnp.where(strict, _dot_bf16(kb, kx, NT_DIMS) * dc, 0.0)
                 for kb, kx, dc in zip(k_beta, k, decay)]
            t_inv = [eye - jnp.where(sib == 1, ax, 0.0) for ax in a]
            s = 2
            while s < c:
                coupling = (sib >= s) & (sib < 2 * s)
                at = [_dot_bf16(jnp.where(coupling, ax, 0.0), tx) for ax, tx in zip(a, t_inv)]
                t_inv = [tx - _dot_bf16(tx, atx) for tx, atx in zip(t_inv, at)]
                s *= 2
            wu = [_dot_bf16(tx, jnp.concatenate([kb * ex, vx * bx], axis=1))
                  for tx, kb, ex, vx, bx in zip(t_inv, k_beta, eg, v, beta)]
            intra = [_dot_bf16(qx, kx, NT_DIMS) * dc for qx, kx, dc in zip(q, k, decay)]
            iwu = [_dot_bf16(ix, wx) for ix, wx in zip(intra, wu)]
            for x in lanes:
                qp_ref[hh, rows[x], :] = (q[x] * eg[x] - iwu[x][:, :GDN_K_DIM]).astype(BF16)
                o0_ref[hh, rows[x], :] = iwu[x][:, GDN_K_DIM:]
            for x in lanes:
                gl_col = jnp.sum(jnp.where(last_of_chunk, g_row[x], 0.0), axis=1, keepdims=True)
                k_dec = k[x] * jnp.exp(gl_col - g_col[x])
                for ci in range(CHUNKS_PER_GROUP):
                    n = (g0 + x) * CHUNKS_PER_GROUP + ci
                    zt = _dot_bf16(wu[x][ci * c:(ci + 1) * c, :], k_dec[ci * c:(ci + 1) * c, :], TN_DIMS)
                    mt_ref[hh, pl.ds(n * GDN_K_DIM, GDN_K_DIM), :] = zt[:GDN_K_DIM].astype(BF16)
                    nt_ref[hh, pl.ds(n * GDN_V_DIM, GDN_V_DIM), :] = zt[GDN_K_DIM:]
                    gl = g_row[x][:, (ci + 1) * c - 1:(ci + 1) * c]
                    egl_ref[hh, pl.ds(n, 1), :] = jnp.broadcast_to(jnp.exp(gl), (1, LANES))

    def step(n, states):
        r0 = pl.multiple_of(n * c, c)
        m0 = pl.multiple_of(n * GDN_K_DIM, GDN_K_DIM)
        new_states = []
        for hh, st in zip(heads, states):
            st_b = st.astype(BF16)
            os_ref[hh, pl.ds(r0, c), :] = (
                lax.dot_general(qp_ref[hh, pl.ds(r0, c), :], st_b, NT_DIMS, preferred_element_type=F32)
                + o0_ref[hh, pl.ds(r0, c), :])
            new_states.append(
                st * egl_ref[hh, pl.ds(n, 1), :]
                - jnp.dot(st_b, mt_ref[hh, pl.ds(m0, GDN_K_DIM), :], preferred_element_type=F32)
                + nt_ref[hh, pl.ds(m0, GDN_V_DIM), :])
        return tuple(new_states)

    zero_state = jnp.zeros((GDN_V_DIM, GDN_K_DIM), F32)
    lax.fori_loop(0, N_CHUNKS, step, tuple(zero_state for _ in heads), unroll=2)

    for hh in heads:
        cols = slice(hh * LANES, (hh + 1) * LANES)
        z = z_ref[:, cols].astype(F32)
        o = _rms_rows(os_ref[hh], norm_ref[...], NORM_EPS) * (z * _sigmoid(z))
        o_ref[:, cols] = o.astype(o_ref.dtype)


def _gdn(proj, conv_w, gates_tm, g_hm, gdn_norm):
    hps = GDN_HEADS_PER_STEP
    width = hps * LANES
    seq_blk = lambda off: pl.BlockSpec((SEQ, width), lambda b, p: (b, off // hps + p))
    conv_blk = lambda off: pl.BlockSpec((CONV_WIDTH, width), lambda b, p: (0, off // hps + p))
    per_head = lambda rows, dtype: pltpu.VMEM((hps, rows, LANES), dtype)
    return pl.pallas_call(
        _gdn_kernel,
        grid=(BATCH, GDN_HEADS // hps),
        in_specs=[
            seq_blk(GDN_Q_BLK), seq_blk(GDN_K_BLK), seq_blk(GDN_V_BLK), seq_blk(GDN_Z_BLK),
            conv_blk(0), conv_blk(GDN_HEADS), conv_blk(2 * GDN_HEADS),
            pl.BlockSpec((SEQ, LANES), lambda b, p: (b, 0)),
            pl.BlockSpec((GDN_HEADS, N_GROUPS, GDN_GROUP), lambda b, p: (0, b, 0)),
            pl.BlockSpec((1, GDN_V_DIM), lambda b, p: (0, 0)),
        ],
        out_specs=pl.BlockSpec((SEQ, width), lambda b, p: (b, p)),
        out_shape=jax.ShapeDtypeStruct((TOKENS, GDN_HEADS * GDN_V_DIM), BF16),
        scratch_shapes=[
            per_head(SEQ, F32),
            per_head(SEQ, F32),
            per_head(SEQ, F32),
            per_head(SEQ, BF16),
            per_head(SEQ, F32),
            per_head(N_CHUNKS * GDN_K_DIM, BF16),
            per_head(N_CHUNKS * GDN_V_DIM, F32),
            per_head(N_CHUNKS, F32),
            per_head(SEQ, F32),
        ],
        compiler_params=_params("arbitrary", "arbitrary"),
        name="gdn",
    )(proj, proj, proj, proj, conv_w, conv_w, conv_w, gates_tm, g_hm, gdn_norm)


def _out_proj_kernel(xa_ref, xb_ref, wa_ref, wb_ref, h_ref, o_ref):
    o_ref[...] = (h_ref[...]
                  + jnp.dot(xa_ref[...], wa_ref[...].astype(BF16), preferred_element_type=F32)
                  + jnp.dot(xb_ref[...], wb_ref[...].astype(BF16), preferred_element_type=F32))


def _out_proj(xa, xb_spec_col, xb, w_stack, idx, h):
    half = w_stack.shape[1] // 2
    return pl.pallas_call(
        _out_proj_kernel,
        grid=(TOKENS // OUT_TM, D_MODEL // OUT_TN),
        in_specs=[
            pl.BlockSpec((OUT_TM, half), lambda i, j: (i, 0)),
            pl.BlockSpec((OUT_TM, half), lambda i, j: (i, xb_spec_col)),
            pl.BlockSpec((None, half, OUT_TN), lambda i, j: (idx, 0, j)),
            pl.BlockSpec((None, half, OUT_TN), lambda i, j: (idx, 1, j)),
            pl.BlockSpec((OUT_TM, OUT_TN), lambda i, j: (i, j)),
        ],
        out_specs=pl.BlockSpec((OUT_TM, OUT_TN), lambda i, j: (i, j)),
        out_shape=jax.ShapeDtypeStruct((TOKENS, D_MODEL), F32),
        compiler_params=_params("arbitrary", "arbitrary"),
        name="out_proj",
    )(xa, xb, w_stack, w_stack, h)


def _mlp_kernel(h_ref, g_ref, wu_ref, wd_ref, gf_ref, o_ref, xn_ref, *, final_norm):
    f = pl.program_id(1)

    @pl.when(f == 0)
    def _():
        x = h_ref[...]
        xn_ref[...] = _rms_rows(x, g_ref[...], NORM_EPS).astype(BF16)
        o_ref[...] = x

    a = jnp.maximum(jnp.dot(xn_ref[...], wu_ref[...].astype(BF16), preferred_element_type=F32), 0.0)
    o_ref[...] += jnp.dot((a * a).astype(BF16), wd_ref[...].astype(BF16), preferred_element_type=F32)

    if final_norm:
        @pl.when(f == pl.num_programs(1) - 1)
        def _():
            o_ref[...] = _rms_rows(o_ref[...], gf_ref[...], NORM_EPS)


def _mlp(h, g, w_up, w_down, layer, g_final, final_norm):
    return pl.pallas_call(
        functools.partial(_mlp_kernel, final_norm=final_norm),
        grid=(TOKENS // MLP_TM, D_FF // MLP_TF),
        in_specs=[
            pl.BlockSpec((MLP_TM, D_MODEL), lambda i, f: (i, 0)),
            pl.BlockSpec((1, D_MODEL), lambda i, f: (0, 0)),
            pl.BlockSpec((None, D_MODEL, MLP_TF), lambda i, f: (layer, 0, f)),
            pl.BlockSpec((None, MLP_TF, D_MODEL), lambda i, f: (layer, f, 0)),
            pl.BlockSpec((1, D_MODEL), lambda i, f: (0, 0)),
        ],
        out_specs=pl.BlockSpec((MLP_TM, D_MODEL), lambda i, f: (i, 0)),
        out_shape=jax.ShapeDtypeStruct((TOKENS, D_MODEL), F32),
        scratch_shapes=[pltpu.VMEM((MLP_TM, D_MODEL), BF16)],
        compiler_params=_params("arbitrary", "arbitrary"),
        name="mlp",
    )(h, g, w_up, w_down, g_final)


def _lanes(vec, offset=0):
    return jnp.zeros((1, LANES), F32).at[0, offset:offset + vec.shape[0]].set(vec.astype(F32))


def _side_weight(wt):
    return jnp.pad(wt, ((0, LANES - wt.shape[0]), (0, 0)))


def _q_col_scale(n, q_cols, scale):
    return jnp.where(jnp.arange(n) < q_cols, scale, 1.0).astype(F32).reshape(1, n)


def _even_mixer(h, g_mix, w_in, idx, conv_w, lam_vecs, da_norm, a_log, dt_bias, gdn_norm, w_out, layer):
    wt_in = jnp.swapaxes(w_in, 1, 2)
    proj, side = _in_proj(h, g_mix, wt_in, idx, EVEN_MAIN,
                          _q_col_scale(EVEN_MAIN, DA_HEADS * 2 * DA_QK_DIM, DA_QK_SCALE),
                          _side_weight(wt_in[idx, EVEN_MAIN:, :]))
    lam_init = 0.8 - 0.6 * math.exp(-0.3 * layer)
    slopes = 2.0 ** (-8.0 * jnp.arange(1, DA_HEADS + 1, dtype=F32) / DA_HEADS)
    o_a = _da_attention(proj, slopes, lam_vecs, da_norm.reshape(1, DA_V_DIM), lam_init)

    gates_tm = _gdn_gates(side, _lanes(a_log), _lanes(dt_bias))
    g_hm = gates_tm[:, :GDN_HEADS].T.reshape(GDN_HEADS, BATCH * N_GROUPS, GDN_GROUP)
    o_b = _gdn(proj, conv_w, gates_tm, g_hm, gdn_norm.reshape(1, GDN_V_DIM))
    return _out_proj(o_a, 0, o_b, w_out, idx, h)


def _odd_mixer(h, g_mix, w_in, idx, b_f, w_out):
    wt_in = jnp.swapaxes(w_in, 1, 2)
    qkv, side = _in_proj(h, g_mix, wt_in, idx, ODD_MAIN,
                         _q_col_scale(ODD_MAIN, FOX_HEADS * FOX_HEAD_DIM, FOX_QK_SCALE),
                         _side_weight(wt_in[idx, ODD_MAIN:, :]))
    c_tm = _fox_gates(side, _lanes(b_f))
    c_hm = c_tm.reshape(BATCH, SEQ, LANES)[:, :, :FOX_HEADS].transpose(0, 2, 1)
    o = _fox_attention(qkv, c_tm, c_hm)
    return _out_proj(o, 1, o, w_out, idx, h)


def kernel(x, norm_mix, norm_mlp, norm_final, w_in_even, conv_w, lam_q1, lam_k1, lam_q2, lam_k2,
           da_norm, gdn_a_log, gdn_dt_bias, gdn_norm, w_out_even, w_in_odd, fox_b_f, w_out_odd,
           w_up, w_down):
    h = x.reshape(TOKENS, D_MODEL)
    g_final = norm_final.reshape(1, D_MODEL)
    for layer in range(DEPTH):
        i = layer // 2
        g_mix = norm_mix[layer].reshape(1, D_MODEL)
        if layer % 2 == 0:
            lam_vecs = jnp.stack([lam_q1[i], lam_k1[i], lam_q2[i], lam_k2[i]]).astype(F32)
            h = _even_mixer(h, g_mix, w_in_even, i, conv_w[i], lam_vecs, da_norm[i], gdn_a_log[i],
                            gdn_dt_bias[i], gdn_norm[i], w_out_even, layer)
        else:
            h = _odd_mixer(h, g_mix, w_in_odd, i, fox_b_f[i], w_out_odd)
        h = _mlp(h, norm_mlp[layer].reshape(1, D_MODEL), w_up, w_down, layer, g_final, layer == DEPTH - 1)
    return h.reshape(BATCH, SEQ, D_MODEL)
```
